```python
import jax, jax.numpy as jnp
from jax import lax
import numpy as np

D_MODEL = 1024
BATCH = 8
SEQ = 4096
DEPTH = 1

MEM_LEN = 256
MIX_WIDTH = D_MODEL
A_WIDTH = MIX_WIDTH // 2
B_WIDTH = MIX_WIDTH - A_WIDTH
A_HEADS = 4
A_HEAD_DIM = A_WIDTH // A_HEADS
B_HEADS = 4
CHUNK = 128
CONV_W = 3
IN_A = 2 * A_WIDTH
IN_B = 3 * B_WIDTH
IN_TOTAL = IN_A + IN_B
X_HEADS = 4
X_HEAD_DIM = D_MODEL // X_HEADS
D_FF = ((8 * D_MODEL // 3 + 255) // 256) * 256
EPS = 1e-6

kernel_name = "hybrid_sgu_shortconv_xattn_layer"


def rms_norm(x, g):
    xf = x.astype(jnp.float32)
    y = xf * lax.rsqrt(jnp.mean(xf * xf, axis=-1, keepdims=True) + EPS)
    return (y * g.astype(jnp.float32)).astype(x.dtype)


def layer_norm(x, g, b):
    xf = x.astype(jnp.float32)
    mu = jnp.mean(xf, axis=-1, keepdims=True)
    var = jnp.mean(jnp.square(xf - mu), axis=-1, keepdims=True)
    y = (xf - mu) * lax.rsqrt(var + EPS)
    return (y * g.astype(jnp.float32) + b.astype(jnp.float32)).astype(x.dtype)


def spatial_gating(a, sgu_ln_g, sgu_ln_b, w_spatial, b_spatial):
    bsz, seq, _ = a.shape
    a = jax.nn.gelu(a)
    u, v = jnp.split(a, 2, axis=-1)
    v = layer_norm(v, sgu_ln_g, sgu_ln_b)
    n_chunks = seq // CHUNK
    v = v.reshape(bsz, n_chunks, CHUNK, A_HEADS, A_HEAD_DIM)
    mask = jnp.tril(jnp.ones((CHUNK, CHUNK), dtype=w_spatial.dtype))
    w = w_spatial * mask[None]
    mixed = jnp.einsum("hts,bnshd->bnthd", w, v)
    mixed = mixed + jnp.transpose(b_spatial)[None, None, :, :, None]
    mixed = mixed.reshape(bsz, seq, A_WIDTH)
    return u * mixed


def short_gated_conv(h, conv_w):
    gate_b, gate_c, val = jnp.split(h, 3, axis=-1)
    z = gate_c * val
    zp = jnp.pad(z, ((0, 0), (CONV_W - 1, 0), (0, 0)))
    seq = z.shape[1]
    conv = (conv_w[0] * zp[:, 0:seq] + conv_w[1] * zp[:, 1:seq + 1]
            + conv_w[2] * zp[:, 2:seq + 2])
    return gate_b * conv


def cross_attention(h, memn, w_q, w_kv, w_o):
    bsz, seq, _ = h.shape
    q = (h @ w_q).reshape(bsz, seq, X_HEADS, X_HEAD_DIM)
    k, v = jnp.split(memn @ w_kv, 2, axis=-1)
    k = k.reshape(bsz, MEM_LEN, X_HEADS, X_HEAD_DIM)
    v = v.reshape(bsz, MEM_LEN, X_HEADS, X_HEAD_DIM)
    scale = X_HEAD_DIM ** -0.5
    s = jnp.einsum("bshd,bmhd->bhsm", q, k).astype(jnp.float32) * scale
    p = jax.nn.softmax(s, axis=-1).astype(v.dtype)
    o = jnp.einsum("bhsm,bmhd->bshd", p, v).reshape(bsz, seq, D_MODEL)
    return o @ w_o


def setup_inputs(seed: int = 0) -> dict:
    key = jax.random.key(seed)
    ks = jax.random.split(key, 24)
    f32 = jnp.float32
    nrm = lambda k, shape, scale: jax.random.normal(k, shape, f32) * scale
    gain = lambda k, n: 1.0 + 0.02 * jax.random.normal(k, (n,), f32)
    return {
        "x": jax.random.normal(ks[0], (BATCH, SEQ, D_MODEL), f32),
        "mem": jax.random.normal(ks[1], (BATCH, MEM_LEN, D_MODEL), f32),
        "ln_mix_g": gain(ks[2], D_MODEL),
        "w_in": nrm(ks[3], (D_MODEL, IN_TOTAL), D_MODEL ** -0.5),
        "sgu_ln_g": gain(ks[4], A_WIDTH),
        "sgu_ln_b": nrm(ks[5], (A_WIDTH,), 0.02),
        "w_spatial": nrm(ks[6], (A_HEADS, CHUNK, CHUNK), CHUNK ** -0.5),
        "b_spatial": 1.0 + nrm(ks[7], (A_HEADS, CHUNK), 0.02),
        "conv_w": nrm(ks[8], (CONV_W, B_WIDTH), CONV_W ** -0.5),
        "grp_norm_a": gain(ks[9], A_WIDTH),
        "grp_norm_b": gain(ks[10], B_WIDTH),
        "w_out": nrm(ks[11], (MIX_WIDTH, D_MODEL), MIX_WIDTH ** -0.5),
        "ln_attn_g": gain(ks[12], D_MODEL),
        "ln_mem_g": gain(ks[13], D_MODEL),
        "w_q": nrm(ks[14], (D_MODEL, D_MODEL), D_MODEL ** -0.5),
        "w_kv": nrm(ks[15], (D_MODEL, 2 * D_MODEL), D_MODEL ** -0.5),
        "w_o": nrm(ks[16], (D_MODEL, D_MODEL), D_MODEL ** -0.5),
        "ln_ffn_g": gain(ks[17], D_MODEL),
        "w_gate_up": nrm(ks[18], (D_MODEL, 2 * D_FF), D_MODEL ** -0.5),
        "w_down": nrm(ks[19], (D_FF, D_MODEL), D_FF ** -0.5),
        "ln_final_g": gain(ks[20], D_MODEL),
    }


def reference(x, mem, ln_mix_g, w_in, sgu_ln_g, sgu_ln_b, w_spatial, b_spatial,
              conv_w, grp_norm_a, grp_norm_b, w_out, ln_attn_g, ln_mem_g,
              w_q, w_kv, w_o, ln_ffn_g, w_gate_up, w_down, ln_final_g):
    memn = rms_norm(mem, ln_mem_g)
    for _ in range(DEPTH):
        h = rms_norm(x, ln_mix_g) @ w_in
        h_a = h[..., :IN_A]
        h_b = h[..., IN_A:]
        y_a = rms_norm(spatial_gating(h_a, sgu_ln_g, sgu_ln_b, w_spatial, b_spatial), grp_norm_a)
        y_b = rms_norm(short_gated_conv(h_b, conv_w), grp_norm_b)
        x = x + jnp.concatenate([y_a, y_b], axis=-1) @ w_out
        x = x + cross_attention(rms_norm(x, ln_attn_g), memn, w_q, w_kv, w_o)
        g, u = jnp.split(rms_norm(x, ln_ffn_g) @ w_gate_up, 2, axis=-1)
        x = x + (jax.nn.silu(g) * u) @ w_down
    return rms_norm(x, ln_final_g)
```

```python
import functools

import jax
import jax.numpy as jnp
from jax import lax
from jax.experimental import pallas as pl
from jax.experimental.pallas import tpu as pltpu

D_MODEL = 1024
MEM_LEN = 256
A_WIDTH = 512
B_WIDTH = 512
A_HEADS = 4
A_HEAD_DIM = A_WIDTH // A_HEADS
CHUNK = 128
IN_A = 2 * A_WIDTH
IN_TOTAL = IN_A + 3 * B_WIDTH
X_HEADS = 4
X_HEAD_DIM = D_MODEL // X_HEADS
D_FF = 2816
EPS = 1e-6

SEQ_TILE = 512
FF_CHUNK = 1408
CARRY_ROWS = 8
VMEM_LIMIT_BYTES = 56 * 1024 * 1024

BF16 = jnp.bfloat16
F32 = jnp.float32


def _dot(a, b):
    return jnp.dot(a, b, preferred_element_type=F32)


def _rms(x, g):
    return x * lax.rsqrt(jnp.mean(x * x, axis=-1, keepdims=True) + EPS) * g


def _kv_kernel(mem_ref, g_ref, wkv_ref, kt_ref, v_ref):
    memn = _rms(mem_ref[0], g_ref[...]).astype(BF16)
    k = _dot(memn, wkv_ref[:, :D_MODEL])
    v = _dot(memn, wkv_ref[:, D_MODEL:])
    kt_ref[0] = (k * (X_HEAD_DIM ** -0.5)).T.astype(BF16)
    v_ref[0] = v.astype(BF16)


def _mixer_attn_kernel(x_ref, kt_ref, v_ref, ln_mix_g, w_in, sgu_g, sgu_b, w_sp, b_sp, conv_w,
                       gn_a, gn_b, w_out, ln_attn_g, w_q, w_o, o_ref, zc_ref):
    ts = x_ref.shape[1]

    @pl.when(pl.program_id(1) == 0)
    def _():
        zc_ref[...] = jnp.zeros_like(zc_ref)

    x = x_ref[0]
    h = _dot(_rms(x, ln_mix_g[...]).astype(BF16), w_in[...])

    a = jax.nn.gelu(h[:, :IN_A])
    u = a[:, :A_WIDTH]
    v = a[:, A_WIDTH:]
    mu = jnp.mean(v, axis=-1, keepdims=True)
    vc = v - mu
    var = jnp.mean(vc * vc, axis=-1, keepdims=True)
    v = (vc * lax.rsqrt(var + EPS) * sgu_g[...] + sgu_b[...]).astype(BF16)
    row = lax.broadcasted_iota(jnp.int32, (CHUNK, CHUNK), 0)
    col = lax.broadcasted_iota(jnp.int32, (CHUNK, CHUNK), 1)
    causal = row >= col
    w_heads = [jnp.where(causal, w_sp[hd], 0.0).astype(BF16) for hd in range(A_HEADS)]
    b_all = b_sp[...]
    chunks = []
    for c in range(ts // CHUNK):
        heads = []
        for hd in range(A_HEADS):
            vch = v[c * CHUNK:(c + 1) * CHUNK, hd * A_HEAD_DIM:(hd + 1) * A_HEAD_DIM]
            heads.append(_dot(w_heads[hd], vch) + b_all[:, hd:hd + 1])
        chunks.append(jnp.concatenate(heads, axis=-1))
    mixed = jnp.concatenate(chunks, axis=0)
    y_a = _rms(u * mixed, gn_a[...])

    gate_b = h[:, IN_A:IN_A + B_WIDTH]
    z = h[:, IN_A + B_WIDTH:IN_A + 2 * B_WIDTH] * h[:, IN_A + 2 * B_WIDTH:]
    prev = zc_ref[...]
    zc_ref[...] = z[ts - CARRY_ROWS:, :]
    r = lax.broadcasted_iota(jnp.int32, (ts, B_WIDTH), 0)
    p1 = prev[CARRY_ROWS - 1:CARRY_ROWS, :]
    p2 = prev[CARRY_ROWS - 2:CARRY_ROWS - 1, :]
    z1 = jnp.where(r == 0, p1, pltpu.roll(z, 1, 0))
    z2 = jnp.where(r == 0, p2, jnp.where(r == 1, p1, pltpu.roll(z, 2, 0)))
    cw = conv_w[...]
    conv = cw[0:1, :] * z2 + cw[1:2, :] * z1 + cw[2:3, :] * z
    y_b = _rms(gate_b * conv, gn_b[...])

    y = jnp.concatenate([y_a, y_b], axis=-1).astype(BF16)
    x = x + _dot(y, w_out[...])

    q = _dot(_rms(x, ln_attn_g[...]).astype(BF16), w_q[...]).astype(BF16)
    outs = []
    for hd in range(X_HEADS):
        sl = slice(hd * X_HEAD_DIM, (hd + 1) * X_HEAD_DIM)
        s = _dot(q[:, sl], kt_ref[0, sl, :])
        p = jnp.exp(s - jnp.max(s, axis=-1, keepdims=True))
        l = jnp.sum(p, axis=-1, keepdims=True)
        outs.append(_dot(p.astype(BF16), v_ref[0, :, sl]) / l)
    o = jnp.concatenate(outs, axis=-1).astype(BF16)
    o_ref[0] = x + _dot(o, w_o[...])


def _ffn_kernel(x_ref, ln_ffn_g, w_gu, w_down, ln_final_g, o_ref):
    x = x_ref[0]
    xn = _rms(x, ln_ffn_g[...]).astype(BF16)
    acc = x
    for c in range(D_FF // FF_CHUNK):
        lo = c * FF_CHUNK
        g = _dot(xn, w_gu[:, lo:lo + FF_CHUNK])
        u = _dot(xn, w_gu[:, D_FF + lo:D_FF + lo + FF_CHUNK])
        act = (jax.nn.silu(g) * u).astype(BF16)
        acc = acc + _dot(act, w_down[lo:lo + FF_CHUNK, :])
    o_ref[0] = _rms(acc, ln_final_g[...])


def _const_spec(shape):
    zeros = (0,) * len(shape)
    return pl.BlockSpec(shape, lambda *_: zeros, pipeline_mode=pl.Buffered(1))


def kernel(x, mem, ln_mix_g, w_in, sgu_ln_g, sgu_ln_b, w_spatial, b_spatial, conv_w, grp_norm_a, grp_norm_b, w_out, ln_attn_g, ln_mem_g, w_q, w_kv, w_o, ln_ffn_g, w_gate_up, w_down, ln_final_g):
    bsz, seq, d = x.shape
    assert d == D_MODEL and seq % SEQ_TILE == 0 and SEQ_TILE % CHUNK == 0
    assert mem.shape == (bsz, MEM_LEN, D_MODEL)
    n_tiles = seq // SEQ_TILE
    row = lambda g: g.reshape(1, -1)

    kt, v = pl.pallas_call(
        _kv_kernel,
        grid=(bsz,),
        in_specs=[
            pl.BlockSpec((1, MEM_LEN, D_MODEL), lambda b: (b, 0, 0)),
            _const_spec((1, D_MODEL)),
            _const_spec((D_MODEL, 2 * D_MODEL)),
        ],
        out_specs=[
            pl.BlockSpec((1, D_MODEL, MEM_LEN), lambda b: (b, 0, 0)),
            pl.BlockSpec((1, MEM_LEN, D_MODEL), lambda b: (b, 0, 0)),
        ],
        out_shape=[
            jax.ShapeDtypeStruct((bsz, D_MODEL, MEM_LEN), BF16),
            jax.ShapeDtypeStruct((bsz, MEM_LEN, D_MODEL), BF16),
        ],
        compiler_params=pltpu.CompilerParams(
            dimension_semantics=("arbitrary",), vmem_limit_bytes=VMEM_LIMIT_BYTES),
        name="kv_proj",
    )(mem, row(ln_mem_g), w_kv.astype(BF16))

    x_spec = pl.BlockSpec((1, SEQ_TILE, D_MODEL), lambda b, s: (b, s, 0))
    x = pl.pallas_call(
        _mixer_attn_kernel,
        grid=(bsz, n_tiles),
        in_specs=[
            x_spec,
            pl.BlockSpec((1, D_MODEL, MEM_LEN), lambda b, s: (b, 0, 0)),
            pl.BlockSpec((1, MEM_LEN, D_MODEL), lambda b, s: (b, 0, 0)),
            _const_spec((1, D_MODEL)),
            _const_spec((D_MODEL, IN_TOTAL)),
            _const_spec((1, A_WIDTH)),
            _const_spec((1, A_WIDTH)),
            _const_spec((A_HEADS, CHUNK, CHUNK)),
            _const_spec((CHUNK, A_HEADS)),
            _const_spec((3, B_WIDTH)),
            _const_spec((1, A_WIDTH)),
            _const_spec((1, B_WIDTH)),
            _const_spec((D_MODEL, D_MODEL)),
            _const_spec((1, D_MODEL)),
            _const_spec((D_MODEL, D_MODEL)),
            _const_spec((D_MODEL, D_MODEL)),
        ],
        out_specs=x_spec,
        out_shape=jax.ShapeDtypeStruct(x.shape, F32),
        scratch_shapes=[pltpu.VMEM((CARRY_ROWS, B_WIDTH), F32)],
        compiler_params=pltpu.CompilerParams(
            dimension_semantics=("arbitrary", "arbitrary"), vmem_limit_bytes=VMEM_LIMIT_BYTES),
        name="mixer_attn",
    )(x, kt, v, row(ln_mix_g), w_in.astype(BF16), row(sgu_ln_g), row(sgu_ln_b), w_spatial,
      b_spatial.T, conv_w, row(grp_norm_a), row(grp_norm_b), w_out.astype(BF16), row(ln_attn_g),
      w_q.astype(BF16), w_o.astype(BF16))

    return pl.pallas_call(
        _ffn_kernel,
        grid=(bsz, n_tiles),
        in_specs=[
            x_spec,
            _const_spec((1, D_MODEL)),
            _const_spec((D_MODEL, 2 * D_FF)),
            _const_spec((D_FF, D_MODEL)),
            _const_spec((1, D_MODEL)),
        ],
        out_specs=x_spec,
        out_shape=jax.ShapeDtypeStruct(x.shape, F32),
        compiler_params=pltpu.CompilerParams(
            dimension_semantics=("arbitrary", "arbitrary"), vmem_limit_bytes=VMEM_LIMIT_BYTES),
        name="ffn",
    )(x, row(ln_ffn_g), w_gate_up.astype(BF16), w_down.astype(BF16), row(ln_final_g))
```

```python
import jax
import jax.numpy as jnp
from jax import lax
from jax.experimental import pallas as pl
from jax.experimental.pallas import tpu as pltpu

D_MODEL = 1024
MEM_LEN = 256
A_WIDTH = 512
B_WIDTH = 512
A_HEADS = 4
A_HEAD_DIM = A_WIDTH // A_HEADS
CHUNK = 128
IN_A = 2 * A_WIDTH
IN_TOTAL = IN_A + 3 * B_WIDTH
X_HEADS = 4
X_HEAD_DIM = D_MODEL // X_HEADS
D_FF = 2816
EPS = 1e-6

MXU_COLS = 256
SEQ_TILE = 1024
ROW_BLOCK = 256
FF_CHUNKS = ((0, 6 * MXU_COLS), (6 * MXU_COLS, 5 * MXU_COLS))
CARRY_ROWS = 8
VMEM_LIMIT_BYTES = 56 * 1024 * 1024

assert sum(w for _, w in FF_CHUNKS) == D_FF and SEQ_TILE % ROW_BLOCK == 0 and ROW_BLOCK % CHUNK == 0

BF16 = jnp.bfloat16
F32 = jnp.float32


def _dot(a, b):
    return jnp.dot(a, b, preferred_element_type=F32)


def _rms(x, g):
    return x * lax.rsqrt(jnp.mean(x * x, axis=-1, keepdims=True) + EPS) * g


def _kv_kernel(mem_ref, g_ref, wkv_ref, kt_ref, v_ref):
    memn = _rms(mem_ref[0], g_ref[...]).astype(BF16)
    k = _dot(memn, wkv_ref[:, :D_MODEL])
    v = _dot(memn, wkv_ref[:, D_MODEL:])
    kt_ref[0] = (k * (X_HEAD_DIM ** -0.5)).T.astype(BF16)
    v_ref[0] = v.astype(BF16)


def _interleave(stage_gens, stagger):
    gens = list(stage_gens)
    live = [True] * len(gens)
    t = 0
    while any(live):
        for i, gen in enumerate(gens):
            if live[i] and t >= stagger * i:
                live[i] = next(gen, None) is not None
        t += 1


def _mixer_attn_stages(x_ref, o_ref, r0, carry, kt_ref, v_ref, ln_mix_g, w_in, sgu_g, sgu_b, w_heads, b_all,
                       conv_w, gn_a, gn_b, w_out, ln_attn_g, w_q, w_o):
    rb = ROW_BLOCK
    x = x_ref[0, r0:r0 + rb, :]
    xn = _rms(x, ln_mix_g[...]).astype(BF16)
    yield True
    h = _dot(xn, w_in[...])
    yield True

    a = jax.nn.gelu(h[:, :IN_A])
    u = a[:, :A_WIDTH]
    v = a[:, A_WIDTH:]
    mu = jnp.mean(v, axis=-1, keepdims=True)
    vc = v - mu
    var = jnp.mean(vc * vc, axis=-1, keepdims=True)
    v = (vc * lax.rsqrt(var + EPS) * sgu_g[...] + sgu_b[...]).astype(BF16)
    chunks = []
    for c in range(rb // CHUNK):
        heads = []
        for hd in range(A_HEADS):
            vch = v[c * CHUNK:(c + 1) * CHUNK, hd * A_HEAD_DIM:(hd + 1) * A_HEAD_DIM]
            heads.append(_dot(w_heads[hd], vch) + b_all[:, hd:hd + 1])
        chunks.append(jnp.concatenate(heads, axis=-1))
    mixed = jnp.concatenate(chunks, axis=0)
    y_a = _rms(u * mixed, gn_a[...])

    gate_b = h[:, IN_A:IN_A + B_WIDTH]
    z = h[:, IN_A + B_WIDTH:IN_A + 2 * B_WIDTH] * h[:, IN_A + 2 * B_WIDTH:]
    prev = carry['z']
    carry['z'] = z[rb - CARRY_ROWS:, :]
    r = lax.broadcasted_iota(jnp.int32, (rb, B_WIDTH), 0)
    p1 = prev[CARRY_ROWS - 1:CARRY_ROWS, :]
    p2 = prev[CARRY_ROWS - 2:CARRY_ROWS - 1, :]
    z1 = jnp.where(r == 0, p1, pltpu.roll(z, 1, 0))
    z2 = jnp.where(r == 0, p2, jnp.where(r == 1, p1, pltpu.roll(z, 2, 0)))
    cw = conv_w[...]
    conv = cw[0:1, :] * z2 + cw[1:2, :] * z1 + cw[2:3, :] * z
    y_b = _rms(gate_b * conv, gn_b[...])

    y = jnp.concatenate([y_a, y_b], axis=-1).astype(BF16)
    yield True

    x = x + _dot(y, w_out[...])
    yield True
    xn = _rms(x, ln_attn_g[...]).astype(BF16)
    yield True

    q = _dot(xn, w_q[...]).astype(BF16)
    yield True
    outs = []
    for hd in range(X_HEADS):
        sl = slice(hd * X_HEAD_DIM, (hd + 1) * X_HEAD_DIM)
        s = _dot(q[:, sl], kt_ref[0, sl, :])
        p = jnp.exp(s - jnp.max(s, axis=-1, keepdims=True))
        l = jnp.sum(p, axis=-1, keepdims=True)
        outs.append(_dot(p.astype(BF16), v_ref[0, :, sl]) / l)
    o = jnp.concatenate(outs, axis=-1).astype(BF16)
    yield True
    o_ref[0, r0:r0 + rb, :] = x + _dot(o, w_o[...])


def _mixer_attn_kernel(x_ref, kt_ref, v_ref, ln_mix_g, w_in, sgu_g, sgu_b, w_sp, b_sp, conv_w,
                       gn_a, gn_b, w_out, ln_attn_g, w_q, w_o, o_ref, zc_ref):
    @pl.when(pl.program_id(1) == 0)
    def _():
        zc_ref[...] = jnp.zeros_like(zc_ref)

    row = lax.broadcasted_iota(jnp.int32, (CHUNK, CHUNK), 0)
    col = lax.broadcasted_iota(jnp.int32, (CHUNK, CHUNK), 1)
    w_heads = [jnp.where(row >= col, w_sp[hd], 0.0).astype(BF16) for hd in range(A_HEADS)]
    b_all = b_sp[...]

    carry = {'z': zc_ref[...]}
    _interleave(
        (_mixer_attn_stages(x_ref, o_ref, r0, carry, kt_ref, v_ref, ln_mix_g, w_in, sgu_g, sgu_b, w_heads,
                            b_all, conv_w, gn_a, gn_b, w_out, ln_attn_g, w_q, w_o)
         for r0 in range(0, x_ref.shape[1], ROW_BLOCK)),
        stagger=1)
    zc_ref[...] = carry['z']


def _ffn_stages(x_ref, o_ref, r0, ln_ffn_g, w_gu, w_down, ln_final_g):
    x = x_ref[0, r0:r0 + ROW_BLOCK, :]
    xn = _rms(x, ln_ffn_g[...]).astype(BF16)
    yield True
    acc = x
    for lo, width in FF_CHUNKS:
        g = _dot(xn, w_gu[:, lo:lo + width])
        u = _dot(xn, w_gu[:, D_FF + lo:D_FF + lo + width])
        yield True
        act = (jax.nn.silu(g) * u).astype(BF16)
        yield True
        acc = acc + _dot(act, w_down[lo:lo + width, :])
    yield True
    o_ref[0, r0:r0 + ROW_BLOCK, :] = _rms(acc, ln_final_g[...])


def _ffn_kernel(x_ref, ln_ffn_g, w_gu, w_down, ln_final_g, o_ref):
    _interleave(
        (_ffn_stages(x_ref, o_ref, r0, ln_ffn_g, w_gu, w_down, ln_final_g)
         for r0 in range(0, x_ref.shape[1], ROW_BLOCK)),
        stagger=1)


def _const_spec(shape):
    zeros = (0,) * len(shape)
    return pl.BlockSpec(shape, lambda *_: zeros, pipeline_mode=pl.Buffered(1))


def kernel(x, mem, ln_mix_g, w_in, sgu_ln_g, sgu_ln_b, w_spatial, b_spatial, conv_w, grp_norm_a, grp_norm_b, w_out, ln_attn_g, ln_mem_g, w_q, w_kv, w_o, ln_ffn_g, w_gate_up, w_down, ln_final_g):
    bsz, seq, d = x.shape
    assert d == D_MODEL and seq % SEQ_TILE == 0
    assert mem.shape == (bsz, MEM_LEN, D_MODEL)
    n_tiles = seq // SEQ_TILE
    row = lambda g: g.reshape(1, -1)

    kt, v = pl.pallas_call(
        _kv_kernel,
        grid=(bsz,),
        in_specs=[
            pl.BlockSpec((1, MEM_LEN, D_MODEL), lambda b: (b, 0, 0)),
            _const_spec((1, D_MODEL)),
            _const_spec((D_MODEL, 2 * D_MODEL)),
        ],
        out_specs=[
            pl.BlockSpec((1, D_MODEL, MEM_LEN), lambda b: (b, 0, 0)),
            pl.BlockSpec((1, MEM_LEN, D_MODEL), lambda b: (b, 0, 0)),
        ],
        out_shape=[
            jax.ShapeDtypeStruct((bsz, D_MODEL, MEM_LEN), BF16),
            jax.ShapeDtypeStruct((bsz, MEM_LEN, D_MODEL), BF16),
        ],
        compiler_params=pltpu.CompilerParams(
            dimension_semantics=("arbitrary",), vmem_limit_bytes=VMEM_LIMIT_BYTES),
        name="kv_proj",
    )(mem, row(ln_mem_g), w_kv.astype(BF16))

    x_spec = pl.BlockSpec((1, SEQ_TILE, D_MODEL), lambda b, s: (b, s, 0))
    x = pl.pallas_call(
        _mixer_attn_kernel,
        grid=(bsz, n_tiles),
        in_specs=[
            x_spec,
            pl.BlockSpec((1, D_MODEL, MEM_LEN), lambda b, s: (b, 0, 0)),
            pl.BlockSpec((1, MEM_LEN, D_MODEL), lambda b, s: (b, 0, 0)),
            _const_spec((1, D_MODEL)),
            _const_spec((D_MODEL, IN_TOTAL)),
            _const_spec((1, A_WIDTH)),
            _const_spec((1, A_WIDTH)),
            _const_spec((A_HEADS, CHUNK, CHUNK)),
            _const_spec((CHUNK, A_HEADS)),
            _const_spec((3, B_WIDTH)),
            _const_spec((1, A_WIDTH)),
            _const_spec((1, B_WIDTH)),
            _const_spec((D_MODEL, D_MODEL)),
            _const_spec((1, D_MODEL)),
            _const_spec((D_MODEL, D_MODEL)),
            _const_spec((D_MODEL, D_MODEL)),
        ],
        out_specs=x_spec,
        out_shape=jax.ShapeDtypeStruct(x.shape, F32),
        scratch_shapes=[pltpu.VMEM((CARRY_ROWS, B_WIDTH), F32)],
        compiler_params=pltpu.CompilerParams(
            dimension_semantics=("arbitrary", "arbitrary"), vmem_limit_bytes=VMEM_LIMIT_BYTES),
        name="mixer_attn",
    )(x, kt, v, row(ln_mix_g), w_in.astype(BF16), row(sgu_ln_g), row(sgu_ln_b), w_spatial,
      b_spatial.T, conv_w, row(grp_norm_a), row(grp_norm_b), w_out.astype(BF16), row(ln_attn_g),
      w_q.astype(BF16), w_o.astype(BF16))

    return pl.pallas_call(
        _ffn_kernel,
        grid=(bsz, n_tiles),
        in_specs=[
            x_spec,
            _const_spec((1, D_MODEL)),
            _const_spec((D_MODEL, 2 * D_FF)),
            _const_spec((D_FF, D_MODEL)),
            _const_spec((1, D_MODEL)),
        ],
        out_specs=x_spec,
        out_shape=jax.ShapeDtypeStruct(x.shape, F32),
        compiler_params=pltpu.CompilerParams(
            dimension_semantics=("arbitrary", "arbitrary"), vmem_limit_bytes=VMEM_LIMIT_BYTES),
        name="ffn",
    )(x, row(ln_ffn_g), w_gate_up.astype(BF16), w_down.astype(BF16), row(ln_final_g))
```

```python
import functools
import math

import jax
import jax.numpy as jnp
from jax import lax
from jax.experimental import pallas as pl
from jax.experimental.pallas import tpu as pltpu

D_MODEL = 1024
MEM_LEN = 256
A_WIDTH = 512
B_WIDTH = 512
A_HEADS = 4
A_HEAD_DIM = A_WIDTH // A_HEADS
CHUNK = 128
IN_A = 2 * A_WIDTH
IN_TOTAL = IN_A + 3 * B_WIDTH
X_HEADS = 4
X_HEAD_DIM = D_MODEL // X_HEADS
D_FF = 2816
EPS = 1e-6

MXU_COLS = 256
SEQ_TILE = 1024
ROW_BLOCK = 512
FFN_ROW_BLOCK = 256
FF_CHUNKS = ((0, 6 * MXU_COLS), (6 * MXU_COLS, 5 * MXU_COLS))
CARRY_ROWS = 8
VMEM_LIMIT_BYTES = 56 * 1024 * 1024

assert sum(w for _, w in FF_CHUNKS) == D_FF and SEQ_TILE == 2 * ROW_BLOCK and ROW_BLOCK % CHUNK == 0

COLS_U = (0, A_WIDTH)
COLS_V = (A_WIDTH, IN_A)
COLS_GATE_B = (IN_A, IN_A + B_WIDTH)
COLS_GATE_C = (IN_A + B_WIDTH, IN_A + 2 * B_WIDTH)
COLS_VAL = (IN_A + 2 * B_WIDTH, IN_TOTAL)

BF16 = jnp.bfloat16
F32 = jnp.float32


_dot = functools.partial(jnp.dot, preferred_element_type=F32)


def _gelu_tanh(x):
    c = math.sqrt(2.0 / math.pi)
    k = 0.044715
    log2e = 1.0 / math.log(2.0)
    t = x * ((-2.0 * c * log2e) + (-2.0 * c * k * log2e) * (x * x))
    return x / (1.0 + jnp.exp2(t))


def _inv_rms(x):
    return lax.rsqrt(jnp.mean(x * x, axis=-1, keepdims=True) + EPS)


def _rms(x, g):
    return x * _inv_rms(x) * g


def _kv_kernel(mem_ref, g_ref, wkv_ref, kt_ref, v_ref):
    memn = _rms(mem_ref[0], g_ref[...]).astype(BF16)
    k = _dot(memn, wkv_ref[:, :D_MODEL])
    v = _dot(memn, wkv_ref[:, D_MODEL:])
    kt_ref[0] = (k * (X_HEAD_DIM ** -0.5)).T.astype(BF16)
    v_ref[0] = v.astype(BF16)


def _interleave(stage_gens, stagger):
    gens = list(stage_gens)
    live = [True] * len(gens)
    t = 0
    while any(live):
        for i, gen in enumerate(gens):
            if live[i] and t >= stagger * i:
                live[i] = next(gen, None) is not None
        t += 1


def _emit_in_order(stage_gens, order):
    gens = list(stage_gens)
    for i in order:
        next(gens[i], None)
    for gen in gens:
        assert next(gen, None) is None


MIXER_STAGE_ORDER = (0, 0, 1, 0, 1, 0, 1, 0, 0, 0, 1, 0, 0, 1, 1, 1, 1, 1)


def _mixer_attn_stages(x_ref, o_ref, zs_ref, r0, kt_ref, v_ref, ln_mix_g, w_in, sgu_g, sgu_b, w_heads,
                       b_all, conv_w, gn_a, gn_b, w_out, ln_attn_g, w_q, w_o):
    rb = ROW_BLOCK
    x = x_ref[0, r0:r0 + rb, :]
    xg = (x * ln_mix_g[...]).astype(BF16)
    inv = _inv_rms(x)
    in_proj = lambda cols: _dot(xg, w_in[:, cols[0]:cols[1]]) * inv
    yield True

    hv = in_proj(COLS_V)
    hu = in_proj(COLS_U)
    hb = in_proj(COLS_GATE_B)
    hc = in_proj(COLS_GATE_C)
    hval = in_proj(COLS_VAL)
    yield True

    v = _gelu_tanh(hv)
    mu = jnp.mean(v, axis=-1, keepdims=True)
    vc = v - mu
    var = jnp.mean(vc * vc, axis=-1, keepdims=True)
    v = (vc * lax.rsqrt(var + EPS) * sgu_g[...] + sgu_b[...]).astype(BF16)
    chunks = []
    for c in range(rb // CHUNK):
        heads = []
        for hd in range(A_HEADS):
            vch = v[c * CHUNK:(c + 1) * CHUNK, hd * A_HEAD_DIM:(hd + 1) * A_HEAD_DIM]
            heads.append(_dot(w_heads[hd], vch) + b_all[:, hd:hd + 1])
        chunks.append(jnp.concatenate(heads, axis=-1))
    mixed = jnp.concatenate(chunks, axis=0)
    yield True

    y_a = _rms(_gelu_tanh(hu) * mixed, gn_a[...]).astype(BF16)
    z = hc * hval
    zs_ref[CARRY_ROWS + r0:CARRY_ROWS + r0 + rb, :] = z
    z1 = zs_ref[CARRY_ROWS + r0 - 1:CARRY_ROWS + r0 - 1 + rb, :]
    z2 = zs_ref[CARRY_ROWS + r0 - 2:CARRY_ROWS + r0 - 2 + rb, :]
    cw = conv_w[...]
    conv = cw[0:1, :] * z2 + cw[1:2, :] * z1 + cw[2:3, :] * z
    y_b = _rms(hb * conv, gn_b[...]).astype(BF16)
    y = jnp.concatenate([y_a, y_b], axis=-1)
    yield True

    x = x + _dot(y, w_out[...])
    yield True

    xg = (x * ln_attn_g[...]).astype(BF16)
    inv = _inv_rms(x)
    yield True

    q = (_dot(xg, w_q[...]) * inv).astype(BF16)
    yield True

    outs = []
    for hd in range(X_HEADS):
        sl = slice(hd * X_HEAD_DIM, (hd + 1) * X_HEAD_DIM)
        s = _dot(q[:, sl], kt_ref[0, sl, :])
        p = jnp.exp(s - jnp.max(s, axis=-1, keepdims=True))
        l = jnp.sum(p, axis=-1, keepdims=True)
        outs.append(_dot(p.astype(BF16), v_ref[0, :, sl]) / l)
    o = jnp.concatenate(outs, axis=-1).astype(BF16)
    yield True

    o_ref[0, r0:r0 + rb, :] = x + _dot(o, w_o[...])


def _mixer_attn_kernel(x_ref, kt_ref, v_ref, ln_mix_g, w_in, sgu_g, sgu_b, w_sp, b_sp, conv_w,
                       gn_a, gn_b, w_out, ln_attn_g, w_q, w_o, o_ref, zs_ref):
    @pl.when(pl.program_id(1) == 0)
    def _():
        zs_ref[0:CARRY_ROWS, :] = jnp.zeros((CARRY_ROWS, B_WIDTH), F32)

    row = lax.broadcasted_iota(jnp.int32, (CHUNK, CHUNK), 0)
    col = lax.broadcasted_iota(jnp.int32, (CHUNK, CHUNK), 1)
    w_heads = [jnp.where(row >= col, w_sp[hd], 0.0).astype(BF16) for hd in range(A_HEADS)]
    b_all = b_sp[...]

    _emit_in_order(
        (_mixer_attn_stages(x_ref, o_ref, zs_ref, r0, kt_ref, v_ref, ln_mix_g, w_in, sgu_g, sgu_b, w_heads,
                            b_all, conv_w, gn_a, gn_b, w_out, ln_attn_g, w_q, w_o)
         for r0 in range(0, x_ref.shape[1], ROW_BLOCK)),
        MIXER_STAGE_ORDER)
    zs_ref[0:CARRY_ROWS, :] = zs_ref[x_ref.shape[1]:x_ref.shape[1] + CARRY_ROWS, :]


def _ffn_stages(x_ref, o_ref, r0, ln_ffn_g, w_gu, w_down, ln_final_g):
    x = x_ref[0, r0:r0 + FFN_ROW_BLOCK, :]
    xg = (x * ln_ffn_g[...]).astype(BF16)
    inv = _inv_rms(x)
    yield True
    acc = x
    for lo, width in FF_CHUNKS:
        g = _dot(xg, w_gu[:, lo:lo + width]) * inv
        u = _dot(xg, w_gu[:, D_FF + lo:D_FF + lo + width]) * inv
        yield True
        act = (jax.nn.silu(g) * u).astype(BF16)
        yield True
        acc = acc + _dot(act, w_down[lo:lo + width, :])
    yield True
    o_ref[0, r0:r0 + FFN_ROW_BLOCK, :] = _rms(acc, ln_final_g[...])


def _ffn_kernel(x_ref, ln_ffn_g, w_gu, w_down, ln_final_g, o_ref):
    _interleave(
        (_ffn_stages(x_ref, o_ref, r0, ln_ffn_g, w_gu, w_down, ln_final_g)
         for r0 in range(0, x_ref.shape[1], FFN_ROW_BLOCK)),
        stagger=1)


def _const_spec(shape):
    zeros = (0,) * len(shape)
    return pl.BlockSpec(shape, lambda *_: zeros, pipeline_mode=pl.Buffered(1))


def kernel(x, mem, ln_mix_g, w_in, sgu_ln_g, sgu_ln_b, w_spatial, b_spatial, conv_w, grp_norm_a, grp_norm_b, w_out, ln_attn_g, ln_mem_g, w_q, w_kv, w_o, ln_ffn_g, w_gate_up, w_down, ln_final_g):
    bsz, seq, d = x.shape
    assert d == D_MODEL and seq % SEQ_TILE == 0
    assert mem.shape == (bsz, MEM_LEN, D_MODEL)
    n_tiles = seq // SEQ_TILE
    row = lambda g: g.reshape(1, -1)

    kt, v = pl.pallas_call(
        _kv_kernel,
        grid=(bsz,),
        in_specs=[
            pl.BlockSpec((1, MEM_LEN, D_MODEL), lambda b: (b, 0, 0)),
            _const_spec((1, D_MODEL)),
            _const_spec((D_MODEL, 2 * D_MODEL)),
        ],
        out_specs=[
            pl.BlockSpec((1, D_MODEL, MEM_LEN), lambda b: (b, 0, 0)),
            pl.BlockSpec((1, MEM_LEN, D_MODEL), lambda b: (b, 0, 0)),
        ],
        out_shape=[
            jax.ShapeDtypeStruct((bsz, D_MODEL, MEM_LEN), BF16),
            jax.ShapeDtypeStruct((bsz, MEM_LEN, D_MODEL), BF16),
        ],
        compiler_params=pltpu.CompilerParams(
            dimension_semantics=("arbitrary",), vmem_limit_bytes=VMEM_LIMIT_BYTES),
        name="kv_proj",
    )(mem, row(ln_mem_g), w_kv.astype(BF16))

    x_spec = pl.BlockSpec((1, SEQ_TILE, D_MODEL), lambda b, s: (b, s, 0))
    x = pl.pallas_call(
        _mixer_attn_kernel,
        grid=(bsz, n_tiles),
        in_specs=[
            x_spec,
            pl.BlockSpec((1, D_MODEL, MEM_LEN), lambda b, s: (b, 0, 0)),
            pl.BlockSpec((1, MEM_LEN, D_MODEL), lambda b, s: (b, 0, 0)),
            _const_spec((1, D_MODEL)),
            _const_spec((D_MODEL, IN_TOTAL)),
            _const_spec((1, A_WIDTH)),
            _const_spec((1, A_WIDTH)),
            _const_spec((A_HEADS, CHUNK, CHUNK)),
            _const_spec((CHUNK, A_HEADS)),
            _const_spec((3, B_WIDTH)),
            _const_spec((1, A_WIDTH)),
            _const_spec((1, B_WIDTH)),
            _const_spec((D_MODEL, D_MODEL)),
            _const_spec((1, D_MODEL)),
            _const_spec((D_MODEL, D_MODEL)),
            _const_spec((D_MODEL, D_MODEL)),
        ],
        out_specs=x_spec,
        out_shape=jax.ShapeDtypeStruct(x.shape, F32),
        scratch_shapes=[pltpu.VMEM((CARRY_ROWS + SEQ_TILE, B_WIDTH), F32)],
        compiler_params=pltpu.CompilerParams(
            dimension_semantics=("arbitrary", "arbitrary"), vmem_limit_bytes=VMEM_LIMIT_BYTES),
        name="mixer_attn",
    )(x, kt, v, row(ln_mix_g), w_in.astype(BF16), row(sgu_ln_g), row(sgu_ln_b), w_spatial,
      b_spatial.T, conv_w, row(grp_norm_a), row(grp_norm_b), w_out.astype(BF16), row(ln_attn_g),
      w_q.astype(BF16), w_o.astype(BF16))

    return pl.pallas_call(
        _ffn_kernel,
        grid=(bsz, n_tiles),
        in_specs=[
            x_spec,
            _const_spec((1, D_MODEL)),
            _const_spec((D_MODEL, 2 * D_FF)),
            _const_spec((D_FF, D_MODEL)),
            _const_spec((1, D_MODEL)),
        ],
        out_specs=x_spec,
        out_shape=jax.ShapeDtypeStruct(x.shape, F32),
        compiler_params=pltpu.CompilerParams(
            dimension_semantics=("arbitrary", "arbitrary"), vmem_limit_bytes=VMEM_LIMIT_BYTES),
        name="ffn",
    )(x, row(ln_ffn_g), w_gate_up.astype(BF16), w_down.astype(BF16), row(ln_final_g))
```

```python
import functools
import math

import jax
import jax.numpy as jnp
from jax import lax
from jax.experimental import pallas as pl
from jax.experimental.pallas import tpu as pltpu

D_MODEL = 1024
MEM_LEN = 256
A_WIDTH = 512
B_WIDTH = 512
A_HEADS = 4
A_HEAD_DIM = A_WIDTH // A_HEADS
CHUNK = 128
IN_A = 2 * A_WIDTH
IN_TOTAL = IN_A + 3 * B_WIDTH
X_HEADS = 4
X_HEAD_DIM = D_MODEL // X_HEADS
D_FF = 2816
EPS = 1e-6

MXU_COLS = 256
SEQ_TILE = 1024
ROW_BLOCK = 512
FFN_ROW_BLOCK = 256
FF_CHUNKS = ((0, 6 * MXU_COLS), (6 * MXU_COLS, 5 * MXU_COLS))
CARRY_ROWS = 8
VMEM_LIMIT_BYTES = 56 * 1024 * 1024

assert sum(w for _, w in FF_CHUNKS) == D_FF and SEQ_TILE == 2 * ROW_BLOCK and ROW_BLOCK % CHUNK == 0

COLS_U = (0, A_WIDTH)
COLS_V = (A_WIDTH, IN_A)
COLS_GATE_B = (IN_A, IN_A + B_WIDTH)
COLS_GATE_C = (IN_A + B_WIDTH, IN_A + 2 * B_WIDTH)
COLS_VAL = (IN_A + 2 * B_WIDTH, IN_TOTAL)

BF16 = jnp.bfloat16
F32 = jnp.float32


_dot = functools.partial(jnp.dot, preferred_element_type=F32)


def _gelu_tanh(x):
    c = math.sqrt(2.0 / math.pi)
    k = 0.044715
    log2e = 1.0 / math.log(2.0)
    t = x * ((-2.0 * c * log2e) + (-2.0 * c * k * log2e) * (x * x))
    return x / (1.0 + jnp.exp2(t))


def _inv_rms(x):
    return lax.rsqrt(jnp.mean(x * x, axis=-1, keepdims=True) + EPS)


def _rms(x, g):
    return x * _inv_rms(x) * g


def _kv_kernel(mem_ref, g_ref, wkv_ref, wq_ref, wo_ref, wk_ref, vo_ref):
    memn = _rms(mem_ref[0], g_ref[...]).astype(BF16)
    k = _dot(memn, wkv_ref[:, :D_MODEL])
    v = _dot(memn, wkv_ref[:, D_MODEL:]).astype(BF16)
    kt = (k * (X_HEAD_DIM ** -0.5)).T.astype(BF16)
    for hd in range(X_HEADS):
        sl = slice(hd * X_HEAD_DIM, (hd + 1) * X_HEAD_DIM)
        ml = slice(hd * MEM_LEN, (hd + 1) * MEM_LEN)
        wk_ref[0, :, ml] = _dot(wq_ref[:, sl], kt[sl, :]).astype(BF16)
        vo_ref[0, ml, :] = _dot(v[:, sl], wo_ref[sl, :]).astype(BF16)


def _interleave(stage_gens, stagger):
    gens = list(stage_gens)
    live = [True] * len(gens)
    t = 0
    while any(live):
        for i, gen in enumerate(gens):
            if live[i] and t >= stagger * i:
                live[i] = next(gen, None) is not None
        t += 1


def _emit_in_order(stage_gens, order):
    gens = list(stage_gens)
    for i in order:
        next(gens[i], None)
    for gen in gens:
        assert next(gen, None) is None


MIXER_STAGE_ORDER = (0, 0, 1, 0, 1, 0, 1, 0, 0, 0, 1, 0, 0, 1, 1, 1, 1, 1)


def _mixer_attn_stages(x_ref, o_ref, zs_ref, r0, wk_ref, vo_ref, ln_mix_g, w_in, sgu_g, sgu_b, w_heads,
                       b_all, conv_w, gn_a, gn_b, w_out, ln_attn_g):
    rb = ROW_BLOCK
    x = x_ref[0, r0:r0 + rb, :]
    xg = (x * ln_mix_g[...]).astype(BF16)
    inv = _inv_rms(x)
    in_proj = lambda cols: _dot(xg, w_in[:, cols[0]:cols[1]]) * inv
    yield True

    hv = in_proj(COLS_V)
    hu = in_proj(COLS_U)
    hb = in_proj(COLS_GATE_B)
    hc = in_proj(COLS_GATE_C)
    hval = in_proj(COLS_VAL)
    yield True

    v = _gelu_tanh(hv)
    mu = jnp.mean(v, axis=-1, keepdims=True)
    vc = v - mu
    var = jnp.mean(vc * vc, axis=-1, keepdims=True)
    v = (vc * lax.rsqrt(var + EPS) * sgu_g[...] + sgu_b[...]).astype(BF16)
    chunks = []
    for c in range(rb // CHUNK):
        heads = []
        for hd in range(A_HEADS):
            vch = v[c * CHUNK:(c + 1) * CHUNK, hd * A_HEAD_DIM:(hd + 1) * A_HEAD_DIM]
            heads.append(_dot(w_heads[hd], vch) + b_all[:, hd:hd + 1])
        chunks.append(jnp.concatenate(heads, axis=-1))
    mixed = jnp.concatenate(chunks, axis=0)
    yield True

    y_a = _rms(_gelu_tanh(hu) * mixed, gn_a[...]).astype(BF16)
    z = hc * hval
    zs_ref[CARRY_ROWS + r0:CARRY_ROWS + r0 + rb, :] = z
    z1 = zs_ref[CARRY_ROWS + r0 - 1:CARRY_ROWS + r0 - 1 + rb, :]
    z2 = zs_ref[CARRY_ROWS + r0 - 2:CARRY_ROWS + r0 - 2 + rb, :]
    cw = conv_w[...]
    conv = cw[0:1, :] * z2 + cw[1:2, :] * z1 + cw[2:3, :] * z
    y_b = _rms(hb * conv, gn_b[...]).astype(BF16)
    y = jnp.concatenate([y_a, y_b], axis=-1)
    yield True

    x = x + _dot(y, w_out[...])
    yield True

    xg = (x * ln_attn_g[...]).astype(BF16)
    inv = _inv_rms(x)
    yield True

    s_all = _dot(xg, wk_ref[0]) * inv
    yield True

    probs = []
    for hd in range(X_HEADS):
        s = s_all[:, hd * MEM_LEN:(hd + 1) * MEM_LEN]
        p = jnp.exp(s - jnp.max(s, axis=-1, keepdims=True))
        probs.append(p / jnp.sum(p, axis=-1, keepdims=True))
    p_all = jnp.concatenate(probs, axis=-1).astype(BF16)
    yield True

    o_ref[0, r0:r0 + rb, :] = x + _dot(p_all, vo_ref[0])


def _mixer_attn_kernel(x_ref, wk_ref, vo_ref, ln_mix_g, w_in, sgu_g, sgu_b, w_sp, b_sp, conv_w,
                       gn_a, gn_b, w_out, ln_attn_g, o_ref, zs_ref):
    @pl.when(pl.program_id(1) == 0)
    def _():
        zs_ref[0:CARRY_ROWS, :] = jnp.zeros((CARRY_ROWS, B_WIDTH), F32)

    row = lax.broadcasted_iota(jnp.int32, (CHUNK, CHUNK), 0)
    col = lax.broadcasted_iota(jnp.int32, (CHUNK, CHUNK), 1)
    w_heads = [jnp.where(row >= col, w_sp[hd], 0.0).astype(BF16) for hd in range(A_HEADS)]
    b_all = b_sp[...]

    _emit_in_order(
        (_mixer_attn_stages(x_ref, o_ref, zs_ref, r0, wk_ref, vo_ref, ln_mix_g, w_in, sgu_g, sgu_b, w_heads,
                            b_all, conv_w, gn_a, gn_b, w_out, ln_attn_g)
         for r0 in range(0, x_ref.shape[1], ROW_BLOCK)),
        MIXER_STAGE_ORDER)
    zs_ref[0:CARRY_ROWS, :] = zs_ref[x_ref.shape[1]:x_ref.shape[1] + CARRY_ROWS, :]


def _ffn_stages(x_ref, o_ref, r0, ln_ffn_g, w_gu, w_down, ln_final_g):
    x = x_ref[0, r0:r0 + FFN_ROW_BLOCK, :]
    xg = (x * ln_ffn_g[...]).astype(BF16)
    inv = _inv_rms(x)
    yield True
    acc = x
    for lo, width in FF_CHUNKS:
        g = _dot(xg, w_gu[:, lo:lo + width]) * inv
        u = _dot(xg, w_gu[:, D_FF + lo:D_FF + lo + width]) * inv
        yield True
        act = (jax.nn.silu(g) * u).astype(BF16)
        yield True
        acc = acc + _dot(act, w_down[lo:lo + width, :])
    yield True
    o_ref[0, r0:r0 + FFN_ROW_BLOCK, :] = _rms(acc, ln_final_g[...])


def _ffn_kernel(x_ref, ln_ffn_g, w_gu, w_down, ln_final_g, o_ref):
    _interleave(
        (_ffn_stages(x_ref, o_ref, r0, ln_ffn_g, w_gu, w_down, ln_final_g)
         for r0 in range(0, x_ref.shape[1], FFN_ROW_BLOCK)),
        stagger=1)


def _const_spec(shape):
    zeros = (0,) * len(shape)
    return pl.BlockSpec(shape, lambda *_: zeros, pipeline_mode=pl.Buffered(1))


def kernel(x, mem, ln_mix_g, w_in, sgu_ln_g, sgu_ln_b, w_spatial, b_spatial, conv_w, grp_norm_a, grp_norm_b, w_out, ln_attn_g, ln_mem_g, w_q, w_kv, w_o, ln_ffn_g, w_gate_up, w_down, ln_final_g):
    bsz, seq, d = x.shape
    assert d == D_MODEL and seq % SEQ_TILE == 0
    assert mem.shape == (bsz, MEM_LEN, D_MODEL)
    n_tiles = seq // SEQ_TILE
    row = lambda g: g.reshape(1, -1)

    hm = X_HEADS * MEM_LEN
    wk, vo = pl.pallas_call(
        _kv_kernel,
        grid=(bsz,),
        in_specs=[
            pl.BlockSpec((1, MEM_LEN, D_MODEL), lambda b: (b, 0, 0)),
            _const_spec((1, D_MODEL)),
            _const_spec((D_MODEL, 2 * D_MODEL)),
            _const_spec((D_MODEL, D_MODEL)),
            _const_spec((D_MODEL, D_MODEL)),
        ],
        out_specs=[
            pl.BlockSpec((1, D_MODEL, hm), lambda b: (b, 0, 0)),
            pl.BlockSpec((1, hm, D_MODEL), lambda b: (b, 0, 0)),
        ],
        out_shape=[
            jax.ShapeDtypeStruct((bsz, D_MODEL, hm), BF16),
            jax.ShapeDtypeStruct((bsz, hm, D_MODEL), BF16),
        ],
        compiler_params=pltpu.CompilerParams(
            dimension_semantics=("arbitrary",), vmem_limit_bytes=VMEM_LIMIT_BYTES),
        name="kv_proj",
    )(mem, row(ln_mem_g), w_kv.astype(BF16), w_q.astype(BF16), w_o.astype(BF16))

    x_spec = pl.BlockSpec((1, SEQ_TILE, D_MODEL), lambda b, s: (b, s, 0))
    x = pl.pallas_call(
        _mixer_attn_kernel,
        grid=(bsz, n_tiles),
        in_specs=[
            x_spec,
            pl.BlockSpec((1, D_MODEL, hm), lambda b, s: (b, 0, 0)),
            pl.BlockSpec((1, hm, D_MODEL), lambda b, s: (b, 0, 0)),
            _const_spec((1, D_MODEL)),
            _const_spec((D_MODEL, IN_TOTAL)),
            _const_spec((1, A_WIDTH)),
            _const_spec((1, A_WIDTH)),
            _const_spec((A_HEADS, CHUNK, CHUNK)),
            _const_spec((CHUNK, A_HEADS)),
            _const_spec((3, B_WIDTH)),
            _const_spec((1, A_WIDTH)),
            _const_spec((1, B_WIDTH)),
            _const_spec((D_MODEL, D_MODEL)),
            _const_spec((1, D_MODEL)),
        ],
        out_specs=x_spec,
        out_shape=jax.ShapeDtypeStruct(x.shape, F32),
        scratch_shapes=[pltpu.VMEM((CARRY_ROWS + SEQ_TILE, B_WIDTH), F32)],
        compiler_params=pltpu.CompilerParams(
            dimension_semantics=("arbitrary", "arbitrary"), vmem_limit_bytes=VMEM_LIMIT_BYTES),
        name="mixer_attn",
    )(x, wk, vo, row(ln_mix_g), w_in.astype(BF16), row(sgu_ln_g), row(sgu_ln_b), w_spatial,
      b_spatial.T, conv_w, row(grp_norm_a), row(grp_norm_b), w_out.astype(BF16), row(ln_attn_g))

    return pl.pallas_call(
        _ffn_kernel,
        grid=(bsz, n_tiles),
        in_specs=[
            x_spec,
            _const_spec((1, D_MODEL)),
            _const_spec((D_MODEL, 2 * D_FF)),
            _const_spec((D_FF, D_MODEL)),
            _const_spec((1, D_MODEL)),
        ],
        out_specs=x_spec,
        out_shape=jax.ShapeDtypeStruct(x.shape, F32),
        compiler_params=pltpu.CompilerParams(
            dimension_semantics=("arbitrary", "arbitrary"), vmem_limit_bytes=VMEM_LIMIT_BYTES),
        name="ffn",
    )(x, row(ln_ffn_g), w_gate_up.astype(BF16), w_down.astype(BF16), row(ln_final_g))
```

```python
import functools
import math

import jax
import jax.numpy as jnp
from jax import lax
from jax.experimental import pallas as pl
from jax.experimental.pallas import tpu as pltpu

D_MODEL = 1024
MEM_LEN = 256
A_WIDTH = 512
B_WIDTH = 512
A_HEADS = 4
A_HEAD_DIM = A_WIDTH // A_HEADS
CHUNK = 128
IN_A = 2 * A_WIDTH
IN_TOTAL = IN_A + 3 * B_WIDTH
X_HEADS = 4
X_HEAD_DIM = D_MODEL // X_HEADS
D_FF = 2816
EPS = 1e-6

MXU_COLS = 256
SEQ_TILE = 1024
ROW_BLOCK = 512
FFN_ROW_BLOCK = 256
FF_CHUNKS = ((0, 6 * MXU_COLS), (6 * MXU_COLS, 5 * MXU_COLS))
CARRY_ROWS = 8
BF16_SUBLANES = 16
VMEM_LIMIT_BYTES = 56 * 1024 * 1024

assert sum(w for _, w in FF_CHUNKS) == D_FF and SEQ_TILE == 2 * ROW_BLOCK and ROW_BLOCK % CHUNK == 0

COLS_U = (0, A_WIDTH)
COLS_V = (A_WIDTH, IN_A)
COLS_GATE_B = (IN_A, IN_A + B_WIDTH)
COLS_GATE_C = (IN_A + B_WIDTH, IN_A + 2 * B_WIDTH)
COLS_VAL = (IN_A + 2 * B_WIDTH, IN_TOTAL)

BF16 = jnp.bfloat16
F32 = jnp.float32


_dot = functools.partial(jnp.dot, preferred_element_type=F32)


def _gelu_tanh(x):
    c = math.sqrt(2.0 / math.pi)
    k = 0.044715
    log2e = 1.0 / math.log(2.0)
    t = x * ((-2.0 * c * log2e) + (-2.0 * c * k * log2e) * (x * x))
    return x / (1.0 + jnp.exp2(t))


def _inv_rms(x):
    return lax.rsqrt(jnp.mean(x * x, axis=-1, keepdims=True) + EPS)


def _rms(x, g):
    return x * _inv_rms(x) * g


def _cast_chunks(src_refs, dst_refs):
    for src, dst in zip(src_refs, dst_refs):
        dst[...] = src[...].astype(BF16)


def _kv_kernel(mem_ref, g_ref, wkv_ref, wq_ref, wo_ref, w_in32, w_out32, wk_ref, vo_ref, w_in16, w_out16):
    _cast_chunks((w_in32, w_out32), (w_in16, w_out16))
    _kv_fold(mem_ref, g_ref, wkv_ref, wq_ref, wo_ref, wk_ref, vo_ref)


def _kv_fold(mem_ref, g_ref, wkv_ref, wq_ref, wo_ref, wk_ref, vo_ref):
    memn = _rms(mem_ref[0], g_ref[...]).astype(BF16)
    k = _dot(memn, wkv_ref[:, :D_MODEL])
    v = _dot(memn, wkv_ref[:, D_MODEL:]).astype(BF16)
    kt = (k * (X_HEAD_DIM ** -0.5)).T.astype(BF16)
    for hd in range(X_HEADS):
        sl = slice(hd * X_HEAD_DIM, (hd + 1) * X_HEAD_DIM)
        ml = slice(hd * MEM_LEN, (hd + 1) * MEM_LEN)
        wk_ref[0, :, ml] = _dot(wq_ref[:, sl], kt[sl, :]).astype(BF16)
        vo_ref[0, ml, :] = _dot(v[:, sl], wo_ref[sl, :]).astype(BF16)


def _interleave(stage_gens, stagger):
    gens = list(stage_gens)
    live = [True] * len(gens)
    t = 0
    while any(live):
        for i, gen in enumerate(gens):
            if live[i] and t >= stagger * i:
                live[i] = next(gen, None) is not None
        t += 1


def _emit_in_order(stage_gens, order):
    gens = list(stage_gens)
    for i in order:
        next(gens[i], None)
    for gen in gens:
        assert next(gen, None) is None


MIXER_STAGE_ORDER = (0, 0, 1, 0, 1, 0, 1, 0, 0, 0, 1, 0, 0, 1, 1, 1, 1, 1)


def _mixer_attn_stages(x_ref, o_ref, zs_ref, r0, wk_ref, vo_ref, ln_mix_g, w_in, sgu_g, sgu_b, w_heads,
                       b_all, conv_w, gn_a, gn_b, w_out, ln_attn_g):
    rb = ROW_BLOCK
    x = x_ref[0, r0:r0 + rb, :]
    xg = (x * ln_mix_g[...]).astype(BF16)
    inv = _inv_rms(x)
    in_proj = lambda cols: _dot(xg, w_in[:, cols[0]:cols[1]]) * inv
    yield True

    hv = in_proj(COLS_V)
    hu = in_proj(COLS_U)
    hb = in_proj(COLS_GATE_B)
    hc = in_proj(COLS_GATE_C)
    hval = in_proj(COLS_VAL)
    yield True

    v = _gelu_tanh(hv)
    mu = jnp.mean(v, axis=-1, keepdims=True)
    vc = v - mu
    var = jnp.mean(vc * vc, axis=-1, keepdims=True)
    v = (vc * lax.rsqrt(var + EPS) * sgu_g[...] + sgu_b[...]).astype(BF16)
    chunks = []
    for c in range(rb // CHUNK):
        heads = []
        for hd in range(A_HEADS):
            vch = v[c * CHUNK:(c + 1) * CHUNK, hd * A_HEAD_DIM:(hd + 1) * A_HEAD_DIM]
            heads.append(_dot(w_heads[hd], vch) + b_all[:, hd:hd + 1])
        chunks.append(jnp.concatenate(heads, axis=-1))
    mixed = jnp.concatenate(chunks, axis=0)
    yield True

    y_a = _rms(_gelu_tanh(hu) * mixed, gn_a[...]).astype(BF16)
    z = hc * hval
    zs_ref[CARRY_ROWS + r0:CARRY_ROWS + r0 + rb, :] = z
    z1 = zs_ref[CARRY_ROWS + r0 - 1:CARRY_ROWS + r0 - 1 + rb, :]
    z2 = zs_ref[CARRY_ROWS + r0 - 2:CARRY_ROWS + r0 - 2 + rb, :]
    cw = conv_w[...]
    conv = cw[0:1, :] * z2 + cw[1:2, :] * z1 + cw[2:3, :] * z
    y_b = _rms(hb * conv, gn_b[...]).astype(BF16)
    y = jnp.concatenate([y_a, y_b], axis=-1)
    yield True

    x = x + _dot(y, w_out[...])
    yield True

    xg = (x * ln_attn_g[...]).astype(BF16)
    inv = _inv_rms(x)
    yield True

    s_all = _dot(xg, wk_ref[0]) * inv
    yield True

    probs = []
    for hd in range(X_HEADS):
        s = s_all[:, hd * MEM_LEN:(hd + 1) * MEM_LEN]
        p = jnp.exp(s - jnp.max(s, axis=-1, keepdims=True))
        probs.append(p / jnp.sum(p, axis=-1, keepdims=True))
    p_all = jnp.concatenate(probs, axis=-1).astype(BF16)
    yield True

    o_ref[0, r0:r0 + rb, :] = x + _dot(p_all, vo_ref[0])


def _mixer_attn_kernel(x_ref, wk_ref, vo_ref, ln_mix_g, w_in, sgu_g, sgu_b, w_sp, b_sp, conv_w,
                       gn_a, gn_b, w_out, ln_attn_g, w_gu32, w_down32, o_ref, w_gu16, w_down16, zs_ref):
    _cast_chunks((w_gu32, w_down32), (w_gu16, w_down16))

    @pl.when(pl.program_id(1) == 0)
    def _():
        zs_ref[0:CARRY_ROWS, :] = jnp.zeros((CARRY_ROWS, B_WIDTH), F32)

    row = lax.broadcasted_iota(jnp.int32, (CHUNK, CHUNK), 0)
    col = lax.broadcasted_iota(jnp.int32, (CHUNK, CHUNK), 1)
    w_heads = [jnp.where(row >= col, w_sp[hd], 0.0).astype(BF16) for hd in range(A_HEADS)]
    b_all = b_sp[...]

    _emit_in_order(
        (_mixer_attn_stages(x_ref, o_ref, zs_ref, r0, wk_ref, vo_ref, ln_mix_g, w_in, sgu_g, sgu_b, w_heads,
                            b_all, conv_w, gn_a, gn_b, w_out, ln_attn_g)
         for r0 in range(0, x_ref.shape[1], ROW_BLOCK)),
        MIXER_STAGE_ORDER)
    zs_ref[0:CARRY_ROWS, :] = zs_ref[x_ref.shape[1]:x_ref.shape[1] + CARRY_ROWS, :]


def _ffn_stages(x_ref, o_ref, r0, ln_ffn_g, w_gu, w_down, ln_final_g):
    x = x_ref[0, r0:r0 + FFN_ROW_BLOCK, :]
    xg = (x * ln_ffn_g[...]).astype(BF16)
    inv = _inv_rms(x)
    yield True
    acc = x
    for lo, width in FF_CHUNKS:
        g = _dot(xg, w_gu[:, lo:lo + width]) * inv
        u = _dot(xg, w_gu[:, D_FF + lo:D_FF + lo + width]) * inv
        yield True
        act = (jax.nn.silu(g) * u).astype(BF16)
        yield True
        acc = acc + _dot(act, w_down[lo:lo + width, :])
    yield True
    o_ref[0, r0:r0 + FFN_ROW_BLOCK, :] = _rms(acc, ln_final_g[...])


def _ffn_kernel(x_ref, ln_ffn_g, w_gu, w_down, ln_final_g, o_ref):
    _interleave(
        (_ffn_stages(x_ref, o_ref, r0, ln_ffn_g, w_gu, w_down, ln_final_g)
         for r0 in range(0, x_ref.shape[1], FFN_ROW_BLOCK)),
        stagger=1)


def _const_spec(shape):
    zeros = (0,) * len(shape)
    return pl.BlockSpec(shape, lambda *_: zeros, pipeline_mode=pl.Buffered(1))


def kernel(x, mem, ln_mix_g, w_in, sgu_ln_g, sgu_ln_b, w_spatial, b_spatial, conv_w, grp_norm_a, grp_norm_b, w_out, ln_attn_g, ln_mem_g, w_q, w_kv, w_o, ln_ffn_g, w_gate_up, w_down, ln_final_g):
    bsz, seq, d = x.shape
    assert d == D_MODEL and seq % SEQ_TILE == 0
    assert mem.shape == (bsz, MEM_LEN, D_MODEL)
    n_tiles = seq // SEQ_TILE
    row = lambda g: g.reshape(1, -1)

    mix_rows = D_MODEL // bsz
    gu_rows = D_MODEL // (bsz * n_tiles)
    down_steps = 2
    down_rows = D_FF * down_steps // (bsz * n_tiles)
    for rows, total in ((mix_rows, D_MODEL), (gu_rows, D_MODEL), (down_rows, D_FF)):
        assert rows % BF16_SUBLANES == 0 and total % rows == 0
    assert down_rows * (bsz * n_tiles // down_steps) == D_FF

    hm = X_HEADS * MEM_LEN
    wk, vo, w_in16, w_out16 = pl.pallas_call(
        _kv_kernel,
        grid=(bsz,),
        in_specs=[
            pl.BlockSpec((1, MEM_LEN, D_MODEL), lambda b: (b, 0, 0)),
            _const_spec((1, D_MODEL)),
            _const_spec((D_MODEL, 2 * D_MODEL)),
            _const_spec((D_MODEL, D_MODEL)),
            _const_spec((D_MODEL, D_MODEL)),
            pl.BlockSpec((mix_rows, IN_TOTAL), lambda b: (b, 0)),
            pl.BlockSpec((mix_rows, D_MODEL), lambda b: (b, 0)),
        ],
        out_specs=[
            pl.BlockSpec((1, D_MODEL, hm), lambda b: (b, 0, 0)),
            pl.BlockSpec((1, hm, D_MODEL), lambda b: (b, 0, 0)),
            pl.BlockSpec((mix_rows, IN_TOTAL), lambda b: (b, 0)),
            pl.BlockSpec((mix_rows, D_MODEL), lambda b: (b, 0)),
        ],
        out_shape=[
            jax.ShapeDtypeStruct((bsz, D_MODEL, hm), BF16),
            jax.ShapeDtypeStruct((bsz, hm, D_MODEL), BF16),
            jax.ShapeDtypeStruct(w_in.shape, BF16),
            jax.ShapeDtypeStruct(w_out.shape, BF16),
        ],
        compiler_params=pltpu.CompilerParams(
            dimension_semantics=("arbitrary",), vmem_limit_bytes=VMEM_LIMIT_BYTES),
        name="kv_proj",
    )(mem, row(ln_mem_g), w_kv.astype(BF16), w_q.astype(BF16), w_o.astype(BF16), w_in, w_out)

    x_spec = pl.BlockSpec((1, SEQ_TILE, D_MODEL), lambda b, s: (b, s, 0))
    gu_spec = pl.BlockSpec((gu_rows, 2 * D_FF), lambda b, s: (b * n_tiles + s, 0))
    down_spec = pl.BlockSpec((down_rows, D_MODEL), lambda b, s: ((b * n_tiles + s) // down_steps, 0))
    x, w_gu16, w_down16 = pl.pallas_call(
        _mixer_attn_kernel,
        grid=(bsz, n_tiles),
        in_specs=[
            x_spec,
            pl.BlockSpec((1, D_MODEL, hm), lambda b, s: (b, 0, 0)),
            pl.BlockSpec((1, hm, D_MODEL), lambda b, s: (b, 0, 0)),
            _const_spec((1, D_MODEL)),
            _const_spec((D_MODEL, IN_TOTAL)),
            _const_spec((1, A_WIDTH)),
            _const_spec((1, A_WIDTH)),
            _const_spec((A_HEADS, CHUNK, CHUNK)),
            _const_spec((CHUNK, A_HEADS)),
            _const_spec((3, B_WIDTH)),
            _const_spec((1, A_WIDTH)),
            _const_spec((1, B_WIDTH)),
            _const_spec((D_MODEL, D_MODEL)),
            _const_spec((1, D_MODEL)),
            gu_spec,
            down_spec,
        ],
        out_specs=[x_spec, gu_spec, down_spec],
        out_shape=[
            jax.ShapeDtypeStruct(x.shape, F32),
            jax.ShapeDtypeStruct(w_gate_up.shape, BF16),
            jax.ShapeDtypeStruct(w_down.shape, BF16),
        ],
        scratch_shapes=[pltpu.VMEM((CARRY_ROWS + SEQ_TILE, B_WIDTH), F32)],
        compiler_params=pltpu.CompilerParams(
            dimension_semantics=("arbitrary", "arbitrary"), vmem_limit_bytes=VMEM_LIMIT_BYTES),
        name="mixer_attn",
    )(x, wk, vo, row(ln_mix_g), w_in16, row(sgu_ln_g), row(sgu_ln_b), w_spatial,
      b_spatial.T, conv_w, row(grp_norm_a), row(grp_norm_b), w_out16, row(ln_attn_g), w_gate_up, w_down)

    return pl.pallas_call(
        _ffn_kernel,
        grid=(bsz, n_tiles),
        in_specs=[
            x_spec,
            _const_spec((1, D_MODEL)),
            _const_spec((D_MODEL, 2 * D_FF)),
            _const_spec((D_FF, D_MODEL)),
            _const_spec((1, D_MODEL)),
        ],
        out_specs=x_spec,
        out_shape=jax.ShapeDtypeStruct(x.shape, F32),
        compiler_params=pltpu.CompilerParams(
            dimension_semantics=("arbitrary", "arbitrary"), vmem_limit_bytes=VMEM_LIMIT_BYTES),
        name="ffn",
    )(x, row(ln_ffn_g), w_gu16, w_down16, row(ln_final_g))
```

```python
import functools
import math

import jax
import jax.numpy as jnp
from jax import lax
from jax.experimental import pallas as pl
from jax.experimental.pallas import tpu as pltpu

D_MODEL = 1024
MEM_LEN = 256
A_WIDTH = 512
B_WIDTH = 512
A_HEADS = 4
A_HEAD_DIM = A_WIDTH // A_HEADS
CHUNK = 128
IN_A = 2 * A_WIDTH
IN_TOTAL = IN_A + 3 * B_WIDTH
X_HEADS = 4
X_HEAD_DIM = D_MODEL // X_HEADS
D_FF = 2816
EPS = 1e-6

MXU_COLS = 256
SEQ_TILE = 1024
ROW_BLOCK = 512
FFN_ROW_BLOCK = 512
FF_CHUNKS = ((0, 6 * MXU_COLS), (6 * MXU_COLS, 5 * MXU_COLS))
CARRY_ROWS = 8
BF16_SUBLANES = 16
VMEM_LIMIT_BYTES = 56 * 1024 * 1024

assert sum(w for _, w in FF_CHUNKS) == D_FF and SEQ_TILE == 2 * ROW_BLOCK and ROW_BLOCK % CHUNK == 0

COLS_U = (0, A_WIDTH)
COLS_V = (A_WIDTH, IN_A)
COLS_GATE_B = (IN_A, IN_A + B_WIDTH)
COLS_GATE_C = (IN_A + B_WIDTH, IN_A + 2 * B_WIDTH)
COLS_VAL = (IN_A + 2 * B_WIDTH, IN_TOTAL)

BF16 = jnp.bfloat16
F32 = jnp.float32


_dot = functools.partial(jnp.dot, preferred_element_type=F32)


def _gelu_tanh(x):
    c = math.sqrt(2.0 / math.pi)
    k = 0.044715
    log2e = 1.0 / math.log(2.0)
    t = x * ((-2.0 * c * log2e) + (-2.0 * c * k * log2e) * (x * x))
    return x / (1.0 + jnp.exp2(t))


def _shift_rows(z, prev, k):
    rolled = pltpu.roll(z, k, 0)
    r = lax.broadcasted_iota(jnp.int32, prev.shape, 0)
    head = jnp.where(r < k, pltpu.roll(prev, k, 0), rolled[:CARRY_ROWS, :])
    return jnp.concatenate([head, rolled[CARRY_ROWS:, :]], axis=0)


def _inv_rms(x):
    return lax.rsqrt(jnp.mean(x * x, axis=-1, keepdims=True) + EPS)


def _rms(x, g):
    return x * _inv_rms(x) * g


def _cast_chunks(src_refs, dst_refs):
    for src, dst in zip(src_refs, dst_refs):
        dst[...] = src[...].astype(BF16)


def _kv_kernel(mem_ref, g_ref, wkv_ref, wq_ref, wo_ref, w_in32, w_out32, wk_ref, vo_ref, w_in16, w_out16):
    _cast_chunks((w_in32, w_out32), (w_in16, w_out16))
    _kv_fold(mem_ref, g_ref, wkv_ref, wq_ref, wo_ref, wk_ref, vo_ref)


def _kv_fold(mem_ref, g_ref, wkv_ref, wq_ref, wo_ref, wk_ref, vo_ref):
    memn = _rms(mem_ref[0], g_ref[...]).astype(BF16)
    k = _dot(memn, wkv_ref[:, :D_MODEL])
    v = _dot(memn, wkv_ref[:, D_MODEL:]).astype(BF16)
    kt = (k * (X_HEAD_DIM ** -0.5)).T.astype(BF16)
    for hd in range(X_HEADS):
        sl = slice(hd * X_HEAD_DIM, (hd + 1) * X_HEAD_DIM)
        ml = slice(hd * MEM_LEN, (hd + 1) * MEM_LEN)
        wk_ref[0, :, ml] = _dot(wq_ref[:, sl], kt[sl, :]).astype(BF16)
        vo_ref[0, ml, :] = _dot(v[:, sl], wo_ref[sl, :]).astype(BF16)


def _interleave(stage_gens, stagger):
    gens = list(stage_gens)
    live = [True] * len(gens)
    t = 0
    while any(live):
        for i, gen in enumerate(gens):
            if live[i] and t >= stagger * i:
                live[i] = next(gen, None) is not None
        t += 1


def _emit_in_order(stage_gens, order):
    gens = list(stage_gens)
    for i in order:
        next(gens[i], None)
    for gen in gens:
        assert next(gen, None) is None


MIXER_STAGE_ORDER = (0, 0, 1, 0, 1, 0, 1, 0, 0, 0, 1, 0, 0, 1, 1, 1, 1, 1)


def _mixer_attn_stages(x_ref, o_ref, carry, r0, wk_ref, vo_ref, ln_mix_g, w_in, sgu_g, sgu_b, w_heads,
                       b_all, conv_w, gn_a, gn_b, w_out, ln_attn_g):
    rb = ROW_BLOCK
    x = x_ref[0, r0:r0 + rb, :]
    xg = (x * ln_mix_g[...]).astype(BF16)
    inv = _inv_rms(x)
    in_proj = lambda cols: _dot(xg, w_in[:, cols[0]:cols[1]]) * inv
    yield True

    hv = in_proj(COLS_V)
    hu = in_proj(COLS_U)
    hb = in_proj(COLS_GATE_B)
    hc = in_proj(COLS_GATE_C)
    hval = in_proj(COLS_VAL)
    yield True

    v = _gelu_tanh(hv)
    mu = jnp.mean(v, axis=-1, keepdims=True)
    vc = v - mu
    var = jnp.mean(vc * vc, axis=-1, keepdims=True)
    v = (vc * lax.rsqrt(var + EPS) * sgu_g[...] + sgu_b[...]).astype(BF16)
    chunks = []
    for c in range(rb // CHUNK):
        heads = []
        for hd in range(A_HEADS):
            vch = v[c * CHUNK:(c + 1) * CHUNK, hd * A_HEAD_DIM:(hd + 1) * A_HEAD_DIM]
            heads.append(_dot(w_heads[hd], vch) + b_all[:, hd:hd + 1])
        chunks.append(jnp.concatenate(heads, axis=-1))
    mixed = jnp.concatenate(chunks, axis=0)
    yield True

    y_a = _rms(_gelu_tanh(hu) * mixed, gn_a[...]).astype(BF16)
    z = hc * hval
    prev = carry['z']
    carry['z'] = z[rb - CARRY_ROWS:, :]
    z1 = _shift_rows(z, prev, 1)
    z2 = _shift_rows(z, prev, 2)
    cw = conv_w[...]
    conv = cw[0:1, :] * z2 + cw[1:2, :] * z1 + cw[2:3, :] * z
    y_b = _rms(hb * conv, gn_b[...]).astype(BF16)
    y = jnp.concatenate([y_a, y_b], axis=-1)
    yield True

    x = x + _dot(y, w_out[...])
    yield True

    xg = (x * ln_attn_g[...]).astype(BF16)
    inv = _inv_rms(x)
    yield True

    s_all = _dot(xg, wk_ref[0]) * inv
    yield True

    probs = []
    for hd in range(X_HEADS):
        s = s_all[:, hd * MEM_LEN:(hd + 1) * MEM_LEN]
        p = jnp.exp(s - jnp.max(s, axis=-1, keepdims=True))
        probs.append(p / jnp.sum(p, axis=-1, keepdims=True))
    p_all = jnp.concatenate(probs, axis=-1).astype(BF16)
    yield True

    o_ref[0, r0:r0 + rb, :] = x + _dot(p_all, vo_ref[0])


def _mixer_attn_kernel(x_ref, wk_ref, vo_ref, ln_mix_g, w_in, sgu_g, sgu_b, w_sp, b_sp, conv_w,
                       gn_a, gn_b, w_out, ln_attn_g, w_gu32, w_down32, o_ref, w_gu16, w_down16, zc_ref):
    _cast_chunks((w_gu32, w_down32), (w_gu16, w_down16))

    @pl.when(pl.program_id(1) == 0)
    def _():
        zc_ref[...] = jnp.zeros_like(zc_ref)

    row = lax.broadcasted_iota(jnp.int32, (CHUNK, CHUNK), 0)
    col = lax.broadcasted_iota(jnp.int32, (CHUNK, CHUNK), 1)
    w_heads = [jnp.where(row >= col, w_sp[hd], 0.0).astype(BF16) for hd in range(A_HEADS)]
    b_all = b_sp[...]

    carry = {'z': zc_ref[...]}
    _emit_in_order(
        (_mixer_attn_stages(x_ref, o_ref, carry, r0, wk_ref, vo_ref, ln_mix_g, w_in, sgu_g, sgu_b, w_heads,
                            b_all, conv_w, gn_a, gn_b, w_out, ln_attn_g)
         for r0 in range(0, x_ref.shape[1], ROW_BLOCK)),
        MIXER_STAGE_ORDER)
    zc_ref[...] = carry['z']


def _ffn_stages(x_ref, o_ref, r0, ln_ffn_g, w_gu, w_down, ln_final_g):
    x = x_ref[0, r0:r0 + FFN_ROW_BLOCK, :]
    xg = (x * ln_ffn_g[...]).astype(BF16)
    inv = _inv_rms(x)
    yield True
    acc = x
    for lo, width in FF_CHUNKS:
        g = _dot(xg, w_gu[:, lo:lo + width]) * inv
        u = _dot(xg, w_gu[:, D_FF + lo:D_FF + lo + width]) * inv
        yield True
        act = (jax.nn.silu(g) * u).astype(BF16)
        yield True
        acc = acc + _dot(act, w_down[lo:lo + width, :])
    yield True
    o_ref[0, r0:r0 + FFN_ROW_BLOCK, :] = _rms(acc, ln_final_g[...])


def _ffn_kernel(x_ref, ln_ffn_g, w_gu, w_down, ln_final_g, o_ref):
    _interleave(
        (_ffn_stages(x_ref, o_ref, r0, ln_ffn_g, w_gu, w_down, ln_final_g)
         for r0 in range(0, x_ref.shape[1], FFN_ROW_BLOCK)),
        stagger=1)


def _const_spec(shape):
    zeros = (0,) * len(shape)
    return pl.BlockSpec(shape, lambda *_: zeros, pipeline_mode=pl.Buffered(1))


def kernel(x, mem, ln_mix_g, w_in, sgu_ln_g, sgu_ln_b, w_spatial, b_spatial, conv_w, grp_norm_a, grp_norm_b, w_out, ln_attn_g, ln_mem_g, w_q, w_kv, w_o, ln_ffn_g, w_gate_up, w_down, ln_final_g):
    bsz, seq, d = x.shape
    assert d == D_MODEL and seq % SEQ_TILE == 0
    assert mem.shape == (bsz, MEM_LEN, D_MODEL)
    n_tiles = seq // SEQ_TILE
    row = lambda g: g.reshape(1, -1)

    mix_rows = D_MODEL // bsz
    gu_rows = D_MODEL // (bsz * n_tiles)
    down_steps = 2
    down_rows = D_FF * down_steps // (bsz * n_tiles)
    for rows, total in ((mix_rows, D_MODEL), (gu_rows, D_MODEL), (down_rows, D_FF)):
        assert rows % BF16_SUBLANES == 0 and total % rows == 0
    assert down_rows * (bsz * n_tiles // down_steps) == D_FF

    hm = X_HEADS * MEM_LEN
    wk, vo, w_in16, w_out16 = pl.pallas_call(
        _kv_kernel,
        grid=(bsz,),
        in_specs=[
            pl.BlockSpec((1, MEM_LEN, D_MODEL), lambda b: (b, 0, 0)),
            _const_spec((1, D_MODEL)),
            _const_spec((D_MODEL, 2 * D_MODEL)),
            _const_spec((D_MODEL, D_MODEL)),
            _const_spec((D_MODEL, D_MODEL)),
            pl.BlockSpec((mix_rows, IN_TOTAL), lambda b: (b, 0)),
            pl.BlockSpec((mix_rows, D_MODEL), lambda b: (b, 0)),
        ],
        out_specs=[
            pl.BlockSpec((1, D_MODEL, hm), lambda b: (b, 0, 0)),
            pl.BlockSpec((1, hm, D_MODEL), lambda b: (b, 0, 0)),
            pl.BlockSpec((mix_rows, IN_TOTAL), lambda b: (b, 0)),
            pl.BlockSpec((mix_rows, D_MODEL), lambda b: (b, 0)),
        ],
        out_shape=[
            jax.ShapeDtypeStruct((bsz, D_MODEL, hm), BF16),
            jax.ShapeDtypeStruct((bsz, hm, D_MODEL), BF16),
            jax.ShapeDtypeStruct(w_in.shape, BF16),
            jax.ShapeDtypeStruct(w_out.shape, BF16),
        ],
        compiler_params=pltpu.CompilerParams(
            dimension_semantics=("arbitrary",), vmem_limit_bytes=VMEM_LIMIT_BYTES),
        name="kv_proj",
    )(mem, row(ln_mem_g), w_kv.astype(BF16), w_q.astype(BF16), w_o.astype(BF16), w_in, w_out)

    x_spec = pl.BlockSpec((1, SEQ_TILE, D_MODEL), lambda b, s: (b, s, 0))
    gu_spec = pl.BlockSpec((gu_rows, 2 * D_FF), lambda b, s: (b * n_tiles + s, 0))
    down_spec = pl.BlockSpec((down_rows, D_MODEL), lambda b, s: ((b * n_tiles + s) // down_steps, 0))
    x, w_gu16, w_down16 = pl.pallas_call(
        _mixer_attn_kernel,
        grid=(bsz, n_tiles),
        in_specs=[
            x_spec,
            pl.BlockSpec((1, D_MODEL, hm), lambda b, s: (b, 0, 0)),
            pl.BlockSpec((1, hm, D_MODEL), lambda b, s: (b, 0, 0)),
            _const_spec((1, D_MODEL)),
            _const_spec((D_MODEL, IN_TOTAL)),
            _const_spec((1, A_WIDTH)),
            _const_spec((1, A_WIDTH)),
            _const_spec((A_HEADS, CHUNK, CHUNK)),
            _const_spec((CHUNK, A_HEADS)),
            _const_spec((3, B_WIDTH)),
            _const_spec((1, A_WIDTH)),
            _const_spec((1, B_WIDTH)),
            _const_spec((D_MODEL, D_MODEL)),
            _const_spec((1, D_MODEL)),
            gu_spec,
            down_spec,
        ],
        out_specs=[x_spec, gu_spec, down_spec],
        out_shape=[
            jax.ShapeDtypeStruct(x.shape, F32),
            jax.ShapeDtypeStruct(w_gate_up.shape, BF16),
            jax.ShapeDtypeStruct(w_down.shape, BF16),
        ],
        scratch_shapes=[pltpu.VMEM((CARRY_ROWS, B_WIDTH), F32)],
        compiler_params=pltpu.CompilerParams(
            dimension_semantics=("arbitrary", "arbitrary"), vmem_limit_bytes=VMEM_LIMIT_BYTES),
        name="mixer_attn",
    )(x, wk, vo, row(ln_mix_g), w_in16, row(sgu_ln_g), row(sgu_ln_b), w_spatial,
      b_spatial.T, conv_w, row(grp_norm_a), row(grp_norm_b), w_out16, row(ln_attn_g), w_gate_up, w_down)

    return pl.pallas_call(
        _ffn_kernel,
        grid=(bsz, n_tiles),
        in_specs=[
            x_spec,
            _const_spec((1, D_MODEL)),
            _const_spec((D_MODEL, 2 * D_FF)),
            _const_spec((D_FF, D_MODEL)),
            _const_spec((1, D_MODEL)),
        ],
        out_specs=x_spec,
        out_shape=jax.ShapeDtypeStruct(x.shape, F32),
        compiler_params=pltpu.CompilerParams(
            dimension_semantics=("arbitrary", "arbitrary"), vmem_limit_bytes=VMEM_LIMIT_BYTES),
        name="ffn",
    )(x, row(ln_ffn_g), w_gu16, w_down16, row(ln_final_g))
```

```python
import functools
import math

import jax
import jax.numpy as jnp
from jax import lax
from jax.experimental import pallas as pl
from jax.experimental.pallas import tpu as pltpu

D_MODEL = 1024
MEM_LEN = 256
A_WIDTH = 512
B_WIDTH = 512
A_HEADS = 4
A_HEAD_DIM = A_WIDTH // A_HEADS
CHUNK = 128
IN_A = 2 * A_WIDTH
IN_TOTAL = IN_A + 3 * B_WIDTH
X_HEADS = 4
X_HEAD_DIM = D_MODEL // X_HEADS
D_FF = 2816
EPS = 1e-6

MXU_COLS = 256
SEQ_TILE = 1024
ROW_BLOCK = 512
FFN_SEQ_TILE = 2048
FFN_ROW_BLOCK = 256
FF_CHUNKS = ((0, 6 * MXU_COLS), (6 * MXU_COLS, 5 * MXU_COLS))
CARRY_ROWS = 8
BF16_SUBLANES = 16
VMEM_LIMIT_BYTES = 56 * 1024 * 1024
FFN_VMEM_LIMIT_BYTES = 62 * 1024 * 1024

assert sum(w for _, w in FF_CHUNKS) == D_FF and SEQ_TILE == 2 * ROW_BLOCK and ROW_BLOCK % CHUNK == 0

COLS_U = (0, A_WIDTH)
COLS_V = (A_WIDTH, IN_A)
COLS_GATE_B = (IN_A, IN_A + B_WIDTH)
COLS_GATE_C = (IN_A + B_WIDTH, IN_A + 2 * B_WIDTH)
COLS_VAL = (IN_A + 2 * B_WIDTH, IN_TOTAL)

BF16 = jnp.bfloat16
F32 = jnp.float32


_dot = functools.partial(jnp.dot, preferred_element_type=F32)


def _gelu_tanh(x):
    c = math.sqrt(2.0 / math.pi)
    k = 0.044715
    log2e = 1.0 / math.log(2.0)
    t = x * ((-2.0 * c * log2e) + (-2.0 * c * k * log2e) * (x * x))
    return x / (1.0 + jnp.exp2(t))


def _shift_rows(z, prev, k):
    rolled = pltpu.roll(z, k, 0)
    r = lax.broadcasted_iota(jnp.int32, prev.shape, 0)
    head = jnp.where(r < k, pltpu.roll(prev, k, 0), rolled[:CARRY_ROWS, :])
    return jnp.concatenate([head, rolled[CARRY_ROWS:, :]], axis=0)


def _inv_rms(x):
    return lax.rsqrt(jnp.mean(x * x, axis=-1, keepdims=True) + EPS)


def _rms(x, g):
    return x * _inv_rms(x) * g


def _cast_chunks(src_refs, dst_refs):
    for src, dst in zip(src_refs, dst_refs):
        dst[...] = src[...].astype(BF16)


def _kv_kernel(mem_ref, g_ref, wkv_ref, wq_ref, wo_ref, w_in32, w_out32, wk_ref, vo_ref, w_in16, w_out16):
    _cast_chunks((w_in32, w_out32), (w_in16, w_out16))
    _kv_fold(mem_ref, g_ref, wkv_ref, wq_ref, wo_ref, wk_ref, vo_ref)


def _kv_fold(mem_ref, g_ref, wkv_ref, wq_ref, wo_ref, wk_ref, vo_ref):
    memn = _rms(mem_ref[0], g_ref[...]).astype(BF16)
    k = _dot(memn, wkv_ref[:, :D_MODEL])
    v = _dot(memn, wkv_ref[:, D_MODEL:]).astype(BF16)
    kt = (k * (X_HEAD_DIM ** -0.5)).T.astype(BF16)
    for hd in range(X_HEADS):
        sl = slice(hd * X_HEAD_DIM, (hd + 1) * X_HEAD_DIM)
        ml = slice(hd * MEM_LEN, (hd + 1) * MEM_LEN)
        wk_ref[0, :, ml] = _dot(wq_ref[:, sl], kt[sl, :]).astype(BF16)
        vo_ref[0, ml, :] = _dot(v[:, sl], wo_ref[sl, :]).astype(BF16)


def _interleave(stage_gens, stagger):
    gens = list(stage_gens)
    live = [True] * len(gens)
    t = 0
    while any(live):
        for i, gen in enumerate(gens):
            if live[i] and t >= stagger * i:
                live[i] = next(gen, None) is not None
        t += 1


def _emit_in_order(stage_gens, order):
    gens = list(stage_gens)
    for i in order:
        next(gens[i], None)
    for gen in gens:
        assert next(gen, None) is None


MIXER_STAGE_ORDER = (0, 0, 1, 0, 1, 0, 1, 0, 0, 0, 1, 0, 0, 1, 1, 1, 1, 1)


def _mixer_attn_stages(x_ref, o_ref, carry, r0, wk_ref, vo_ref, ln_mix_g, w_in, sgu_g, sgu_b, w_heads,
                       b_all, conv_w, gn_a, gn_b, w_out, ln_attn_g):
    rb = ROW_BLOCK
    x = x_ref[0, r0:r0 + rb, :]
    xg = (x * ln_mix_g[...]).astype(BF16)
    inv = _inv_rms(x)
    in_proj = lambda cols: _dot(xg, w_in[:, cols[0]:cols[1]]) * inv
    yield True

    hv = in_proj(COLS_V)
    hu = in_proj(COLS_U)
    hb = in_proj(COLS_GATE_B)
    hc = in_proj(COLS_GATE_C)
    hval = in_proj(COLS_VAL)
    yield True

    v = _gelu_tanh(hv)
    mu = jnp.mean(v, axis=-1, keepdims=True)
    vc = v - mu
    var = jnp.mean(vc * vc, axis=-1, keepdims=True)
    v = (vc * lax.rsqrt(var + EPS) * sgu_g[...] + sgu_b[...]).astype(BF16)
    chunks = []
    for c in range(rb // CHUNK):
        heads = []
        for hd in range(A_HEADS):
            vch = v[c * CHUNK:(c + 1) * CHUNK, hd * A_HEAD_DIM:(hd + 1) * A_HEAD_DIM]
            heads.append(_dot(w_heads[hd], vch) + b_all[:, hd:hd + 1])
        chunks.append(jnp.concatenate(heads, axis=-1))
    mixed = jnp.concatenate(chunks, axis=0)
    yield True

    y_a = _rms(_gelu_tanh(hu) * mixed, gn_a[...]).astype(BF16)
    z = hc * hval
    prev = carry['z']
    carry['z'] = z[rb - CARRY_ROWS:, :]
    z1 = _shift_rows(z, prev, 1)
    z2 = _shift_rows(z, prev, 2)
    cw = conv_w[...]
    conv = cw[0:1, :] * z2 + cw[1:2, :] * z1 + cw[2:3, :] * z
    y_b = _rms(hb * conv, gn_b[...]).astype(BF16)
    y = jnp.concatenate([y_a, y_b], axis=-1)
    yield True

    x = x + _dot(y, w_out[...])
    yield True

    xg = (x * ln_attn_g[...]).astype(BF16)
    inv = _inv_rms(x)
    yield True

    s_all = _dot(xg, wk_ref[0]) * inv
    yield True

    probs = []
    for hd in range(X_HEADS):
        s = s_all[:, hd * MEM_LEN:(hd + 1) * MEM_LEN]
        p = jnp.exp(s - jnp.max(s, axis=-1, keepdims=True))
        probs.append(p / jnp.sum(p, axis=-1, keepdims=True))
    p_all = jnp.concatenate(probs, axis=-1).astype(BF16)
    yield True

    o_ref[0, r0:r0 + rb, :] = x + _dot(p_all, vo_ref[0])


def _mixer_attn_kernel(x_ref, wk_ref, vo_ref, ln_mix_g, w_in, sgu_g, sgu_b, w_sp, b_sp, conv_w,
                       gn_a, gn_b, w_out, ln_attn_g, w_gu32, w_down32, o_ref, w_gu16, w_down16, zc_ref):
    _cast_chunks((w_gu32, w_down32), (w_gu16, w_down16))

    @pl.when(pl.program_id(1) == 0)
    def _():
        zc_ref[...] = jnp.zeros_like(zc_ref)

    row = lax.broadcasted_iota(jnp.int32, (CHUNK, CHUNK), 0)
    col = lax.broadcasted_iota(jnp.int32, (CHUNK, CHUNK), 1)
    w_heads = [jnp.where(row >= col, w_sp[hd], 0.0).astype(BF16) for hd in range(A_HEADS)]
    b_all = b_sp[...]

    carry = {'z': zc_ref[...]}
    _emit_in_order(
        (_mixer_attn_stages(x_ref, o_ref, carry, r0, wk_ref, vo_ref, ln_mix_g, w_in, sgu_g, sgu_b, w_heads,
                            b_all, conv_w, gn_a, gn_b, w_out, ln_attn_g)
         for r0 in range(0, x_ref.shape[1], ROW_BLOCK)),
        MIXER_STAGE_ORDER)
    zc_ref[...] = carry['z']


def _ffn_stages(x_ref, o_ref, r0, ln_ffn_g, w_gu, w_down, ln_final_g):
    x = x_ref[0, r0:r0 + FFN_ROW_BLOCK, :]
    xg = (x * ln_ffn_g[...]).astype(BF16)
    inv = _inv_rms(x)
    yield True
    acc = x
    for lo, width in FF_CHUNKS:
        g = _dot(xg, w_gu[:, lo:lo + width]) * inv
        u = _dot(xg, w_gu[:, D_FF + lo:D_FF + lo + width]) * inv
        yield True
        act = (jax.nn.silu(g) * u).astype(BF16)
        yield True
        acc = acc + _dot(act, w_down[lo:lo + width, :])
    yield True
    o_ref[0, r0:r0 + FFN_ROW_BLOCK, :] = _rms(acc, ln_final_g[...])


def _ffn_kernel(x_ref, ln_ffn_g, w_gu, w_down, ln_final_g, o_ref):
    _interleave(
        (_ffn_stages(x_ref, o_ref, r0, ln_ffn_g, w_gu, w_down, ln_final_g)
         for r0 in range(0, x_ref.shape[1], FFN_ROW_BLOCK)),
        stagger=1)


def _const_spec(shape):
    zeros = (0,) * len(shape)
    return pl.BlockSpec(shape, lambda *_: zeros, pipeline_mode=pl.Buffered(1))


def kernel(x, mem, ln_mix_g, w_in, sgu_ln_g, sgu_ln_b, w_spatial, b_spatial, conv_w, grp_norm_a, grp_norm_b, w_out, ln_attn_g, ln_mem_g, w_q, w_kv, w_o, ln_ffn_g, w_gate_up, w_down, ln_final_g):
    bsz, seq, d = x.shape
    assert d == D_MODEL and seq % SEQ_TILE == 0
    assert mem.shape == (bsz, MEM_LEN, D_MODEL)
    n_tiles = seq // SEQ_TILE
    row = lambda g: g.reshape(1, -1)

    mix_rows = D_MODEL // bsz
    gu_rows = D_MODEL // (bsz * n_tiles)
    down_steps = 2
    down_rows = D_FF * down_steps // (bsz * n_tiles)
    for rows, total in ((mix_rows, D_MODEL), (gu_rows, D_MODEL), (down_rows, D_FF)):
        assert rows % BF16_SUBLANES == 0 and total % rows == 0
    assert down_rows * (bsz * n_tiles // down_steps) == D_FF

    hm = X_HEADS * MEM_LEN
    wk, vo, w_in16, w_out16 = pl.pallas_call(
        _kv_kernel,
        grid=(bsz,),
        in_specs=[
            pl.BlockSpec((1, MEM_LEN, D_MODEL), lambda b: (b, 0, 0)),
            _const_spec((1, D_MODEL)),
            _const_spec((D_MODEL, 2 * D_MODEL)),
            _const_spec((D_MODEL, D_MODEL)),
            _const_spec((D_MODEL, D_MODEL)),
            pl.BlockSpec((mix_rows, IN_TOTAL), lambda b: (b, 0)),
            pl.BlockSpec((mix_rows, D_MODEL), lambda b: (b, 0)),
        ],
        out_specs=[
            pl.BlockSpec((1, D_MODEL, hm), lambda b: (b, 0, 0)),
            pl.BlockSpec((1, hm, D_MODEL), lambda b: (b, 0, 0)),
            pl.BlockSpec((mix_rows, IN_TOTAL), lambda b: (b, 0)),
            pl.BlockSpec((mix_rows, D_MODEL), lambda b: (b, 0)),
        ],
        out_shape=[
            jax.ShapeDtypeStruct((bsz, D_MODEL, hm), BF16),
            jax.ShapeDtypeStruct((bsz, hm, D_MODEL), BF16),
            jax.ShapeDtypeStruct(w_in.shape, BF16),
            jax.ShapeDtypeStruct(w_out.shape, BF16),
        ],
        compiler_params=pltpu.CompilerParams(
            dimension_semantics=("arbitrary",), vmem_limit_bytes=VMEM_LIMIT_BYTES),
        name="kv_proj",
    )(mem, row(ln_mem_g), w_kv.astype(BF16), w_q.astype(BF16), w_o.astype(BF16), w_in, w_out)

    x_spec = pl.BlockSpec((1, SEQ_TILE, D_MODEL), lambda b, s: (b, s, 0))
    gu_spec = pl.BlockSpec((gu_rows, 2 * D_FF), lambda b, s: (b * n_tiles + s, 0))
    down_spec = pl.BlockSpec((down_rows, D_MODEL), lambda b, s: ((b * n_tiles + s) // down_steps, 0))
    x, w_gu16, w_down16 = pl.pallas_call(
        _mixer_attn_kernel,
        grid=(bsz, n_tiles),
        in_specs=[
            x_spec,
            pl.BlockSpec((1, D_MODEL, hm), lambda b, s: (b, 0, 0)),
            pl.BlockSpec((1, hm, D_MODEL), lambda b, s: (b, 0, 0)),
            _const_spec((1, D_MODEL)),
            _const_spec((D_MODEL, IN_TOTAL)),
            _const_spec((1, A_WIDTH)),
            _const_spec((1, A_WIDTH)),
            _const_spec((A_HEADS, CHUNK, CHUNK)),
            _const_spec((CHUNK, A_HEADS)),
            _const_spec((3, B_WIDTH)),
            _const_spec((1, A_WIDTH)),
            _const_spec((1, B_WIDTH)),
            _const_spec((D_MODEL, D_MODEL)),
            _const_spec((1, D_MODEL)),
            gu_spec,
            down_spec,
        ],
        out_specs=[x_spec, gu_spec, down_spec],
        out_shape=[
            jax.ShapeDtypeStruct(x.shape, F32),
            jax.ShapeDtypeStruct(w_gate_up.shape, BF16),
            jax.ShapeDtypeStruct(w_down.shape, BF16),
        ],
        scratch_shapes=[pltpu.VMEM((CARRY_ROWS, B_WIDTH), F32)],
        compiler_params=pltpu.CompilerParams(
            dimension_semantics=("arbitrary", "arbitrary"), vmem_limit_bytes=VMEM_LIMIT_BYTES),
        name="mixer_attn",
    )(x, wk, vo, row(ln_mix_g), w_in16, row(sgu_ln_g), row(sgu_ln_b), w_spatial,
      b_spatial.T, conv_w, row(grp_norm_a), row(grp_norm_b), w_out16, row(ln_attn_g), w_gate_up, w_down)

    ffn_x_spec = pl.BlockSpec((1, FFN_SEQ_TILE, D_MODEL), lambda b, s: (b, s, 0))
    return pl.pallas_call(
        _ffn_kernel,
        grid=(bsz, seq // FFN_SEQ_TILE),
        in_specs=[
            ffn_x_spec,
            _const_spec((1, D_MODEL)),
            _const_spec((D_MODEL, 2 * D_FF)),
            _const_spec((D_FF, D_MODEL)),
            _const_spec((1, D_MODEL)),
        ],
        out_specs=ffn_x_spec,
        out_shape=jax.ShapeDtypeStruct(x.shape, F32),
        compiler_params=pltpu.CompilerParams(
            dimension_semantics=("arbitrary", "arbitrary"), vmem_limit_bytes=FFN_VMEM_LIMIT_BYTES),
        name="ffn",
    )(x, row(ln_ffn_g), w_gu16, w_down16, row(ln_final_g))
```

```python
import functools
import math

import jax
import jax.numpy as jnp
from jax import lax
from jax.experimental import pallas as pl
from jax.experimental.pallas import tpu as pltpu

D_MODEL = 1024
MEM_LEN = 256
A_WIDTH = 512
B_WIDTH = 512
A_HEADS = 4
A_HEAD_DIM = A_WIDTH // A_HEADS
CHUNK = 128
IN_A = 2 * A_WIDTH
IN_TOTAL = IN_A + 3 * B_WIDTH
X_HEADS = 4
X_HEAD_DIM = D_MODEL // X_HEADS
D_FF = 2816
EPS = 1e-6

MXU_COLS = 256
SEQ_TILE = 1024
ROW_BLOCK = 512
FFN_SEQ_TILE = 1024
FFN_ROW_BLOCK = 256
FF_CHUNKS = ((0, 6 * MXU_COLS), (6 * MXU_COLS, 5 * MXU_COLS))
CARRY_ROWS = 8
BF16_SUBLANES = 16
VMEM_LIMIT_BYTES = 56 * 1024 * 1024
FFN_VMEM_LIMIT_BYTES = VMEM_LIMIT_BYTES

assert sum(w for _, w in FF_CHUNKS) == D_FF and SEQ_TILE == 2 * ROW_BLOCK and ROW_BLOCK % CHUNK == 0

COLS_U = (0, A_WIDTH)
COLS_V = (A_WIDTH, IN_A)
COLS_GATE_B = (IN_A, IN_A + B_WIDTH)
COLS_GATE_C = (IN_A + B_WIDTH, IN_A + 2 * B_WIDTH)
COLS_VAL = (IN_A + 2 * B_WIDTH, IN_TOTAL)

BF16 = jnp.bfloat16
F32 = jnp.float32


_dot = functools.partial(jnp.dot, preferred_element_type=F32)


def _gelu_tanh(x):
    c = math.sqrt(2.0 / math.pi)
    k = 0.044715
    log2e = 1.0 / math.log(2.0)
    t = x * ((-2.0 * c * log2e) + (-2.0 * c * k * log2e) * (x * x))
    return x / (1.0 + jnp.exp2(t))


def _shift_rows(z, prev, k):
    rolled = pltpu.roll(z, k, 0)
    r = lax.broadcasted_iota(jnp.int32, prev.shape, 0)
    head = jnp.where(r < k, pltpu.roll(prev, k, 0), rolled[:CARRY_ROWS, :])
    return jnp.concatenate([head, rolled[CARRY_ROWS:, :]], axis=0)


def _inv_rms(x):
    return lax.rsqrt(jnp.mean(x * x, axis=-1, keepdims=True) + EPS)


def _rms(x, g):
    return x * _inv_rms(x) * g


def _cast_chunks(src_refs, dst_refs):
    for src, dst in zip(src_refs, dst_refs):
        dst[...] = src[...].astype(BF16)


def _kv_kernel(mem_ref, g_ref, wkv_ref, wq_ref, wo_ref, w_in32, w_out32, wk_ref, vo_ref, w_in16, w_out16):
    _cast_chunks((w_in32, w_out32), (w_in16, w_out16))
    _kv_fold(mem_ref, g_ref, wkv_ref, wq_ref, wo_ref, wk_ref, vo_ref)


def _kv_fold(mem_ref, g_ref, wkv_ref, wq_ref, wo_ref, wk_ref, vo_ref):
    memn = _rms(mem_ref[0], g_ref[...]).astype(BF16)
    k = _dot(memn, wkv_ref[:, :D_MODEL])
    v = _dot(memn, wkv_ref[:, D_MODEL:]).astype(BF16)
    kt = (k * (X_HEAD_DIM ** -0.5)).T.astype(BF16)
    for hd in range(X_HEADS):
        sl = slice(hd * X_HEAD_DIM, (hd + 1) * X_HEAD_DIM)
        ml = slice(hd * MEM_LEN, (hd + 1) * MEM_LEN)
        wk_ref[0, :, ml] = _dot(wq_ref[:, sl], kt[sl, :]).astype(BF16)
        vo_ref[0, ml, :] = _dot(v[:, sl], wo_ref[sl, :]).astype(BF16)


def _interleave(stage_gens, stagger):
    gens = list(stage_gens)
    live = [True] * len(gens)
    t = 0
    while any(live):
        for i, gen in enumerate(gens):
            if live[i] and t >= stagger * i:
                live[i] = next(gen, None) is not None
        t += 1


def _emit_in_order(stage_gens, order):
    gens = list(stage_gens)
    for i in order:
        next(gens[i], None)
    for gen in gens:
        assert next(gen, None) is None


MIXER_STAGE_ORDER = (0, 0, 1, 0, 1, 0, 1, 0, 0, 0, 1, 0, 0, 1, 1, 1, 1, 1)


def _mixer_attn_stages(x_ref, o_ref, carry, r0, wk_ref, vo_ref, ln_mix_g, w_in, sgu_g, sgu_b, w_heads,
                       b_all, conv_w, gn_a, gn_b, w_out, ln_attn_g):
    rb = ROW_BLOCK
    x = x_ref[0, r0:r0 + rb, :]
    xg = (x * ln_mix_g[...]).astype(BF16)
    inv = _inv_rms(x)
    in_proj = lambda cols: _dot(xg, w_in[:, cols[0]:cols[1]]) * inv
    yield True

    hv = in_proj(COLS_V)
    hu = in_proj(COLS_U)
    hb = in_proj(COLS_GATE_B)
    hc = in_proj(COLS_GATE_C)
    hval = in_proj(COLS_VAL)
    yield True

    v = _gelu_tanh(hv)
    mu = jnp.mean(v, axis=-1, keepdims=True)
    vc = v - mu
    var = jnp.mean(vc * vc, axis=-1, keepdims=True)
    v = (vc * lax.rsqrt(var + EPS) * sgu_g[...] + sgu_b[...]).astype(BF16)
    chunks = []
    for c in range(rb // CHUNK):
        heads = []
        for hd in range(A_HEADS):
            vch = v[c * CHUNK:(c + 1) * CHUNK, hd * A_HEAD_DIM:(hd + 1) * A_HEAD_DIM]
            heads.append(_dot(w_heads[hd], vch) + b_all[:, hd:hd + 1])
        chunks.append(jnp.concatenate(heads, axis=-1))
    mixed = jnp.concatenate(chunks, axis=0)
    yield True

    y_a = _rms(_gelu_tanh(hu) * mixed, gn_a[...]).astype(BF16)
    z = hc * hval
    prev = carry['z']
    carry['z'] = z[rb - CARRY_ROWS:, :]
    z1 = _shift_rows(z, prev, 1)
    z2 = _shift_rows(z, prev, 2)
    cw = conv_w[...]
    conv = cw[0:1, :] * z2 + cw[1:2, :] * z1 + cw[2:3, :] * z
    y_b = _rms(hb * conv, gn_b[...]).astype(BF16)
    y = jnp.concatenate([y_a, y_b], axis=-1)
    yield True

    x = x + _dot(y, w_out[...])
    yield True

    xg = (x * ln_attn_g[...]).astype(BF16)
    inv = _inv_rms(x)
    yield True

    s_all = _dot(xg, wk_ref[0]) * inv
    yield True

    probs = []
    for hd in range(X_HEADS):
        s = s_all[:, hd * MEM_LEN:(hd + 1) * MEM_LEN]
        p = jnp.exp(s - jnp.max(s, axis=-1, keepdims=True))
        probs.append(p / jnp.sum(p, axis=-1, keepdims=True))
    p_all = jnp.concatenate(probs, axis=-1).astype(BF16)
    yield True

    o_ref[0, r0:r0 + rb, :] = x + _dot(p_all, vo_ref[0])


def _mixer_attn_kernel(x_ref, wk_ref, vo_ref, ln_mix_g, w_in, sgu_g, sgu_b, w_sp, b_sp, conv_w,
                       gn_a, gn_b, w_out, ln_attn_g, w_gu32, w_down32, o_ref, w_gu16, w_down16, zc_ref):
    _cast_chunks((w_gu32, w_down32), (w_gu16, w_down16))

    @pl.when(pl.program_id(1) == 0)
    def _():
        zc_ref[...] = jnp.zeros_like(zc_ref)

    row = lax.broadcasted_iota(jnp.int32, (CHUNK, CHUNK), 0)
    col = lax.broadcasted_iota(jnp.int32, (CHUNK, CHUNK), 1)
    w_heads = [jnp.where(row >= col, w_sp[hd], 0.0).astype(BF16) for hd in range(A_HEADS)]
    b_all = b_sp[...]

    carry = {'z': zc_ref[...]}
    _emit_in_order(
        (_mixer_attn_stages(x_ref, o_ref, carry, r0, wk_ref, vo_ref, ln_mix_g, w_in, sgu_g, sgu_b, w_heads,
                            b_all, conv_w, gn_a, gn_b, w_out, ln_attn_g)
         for r0 in range(0, x_ref.shape[1], ROW_BLOCK)),
        MIXER_STAGE_ORDER)
    zc_ref[...] = carry['z']


def _ffn_stages(x_ref, o_ref, r0, ln_ffn_g, w_gu, w_down, ln_final_g):
    x = x_ref[0, r0:r0 + FFN_ROW_BLOCK, :]
    xg = (x * ln_ffn_g[...]).astype(BF16)
    inv = _inv_rms(x)
    yield True
    acc = x
    for lo, width in FF_CHUNKS:
        g = _dot(xg, w_gu[:, lo:lo + width]) * inv
        u = _dot(xg, w_gu[:, D_FF + lo:D_FF + lo + width]) * inv
        yield True
        act = (jax.nn.silu(g) * u).astype(BF16)
        yield True
        acc = acc + _dot(act, w_down[lo:lo + width, :])
    yield True
    o_ref[0, r0:r0 + FFN_ROW_BLOCK, :] = _rms(acc, ln_final_g[...])


def _ffn_kernel(x_ref, ln_ffn_g, w_gu, w_down, ln_final_g, o_ref):
    _interleave(
        (_ffn_stages(x_ref, o_ref, r0, ln_ffn_g, w_gu, w_down, ln_final_g)
         for r0 in range(0, x_ref.shape[1], FFN_ROW_BLOCK)),
        stagger=1)


def _const_spec(shape):
    zeros = (0,) * len(shape)
    return pl.BlockSpec(shape, lambda *_: zeros, pipeline_mode=pl.Buffered(1))


def kernel(x, mem, ln_mix_g, w_in, sgu_ln_g, sgu_ln_b, w_spatial, b_spatial, conv_w, grp_norm_a, grp_norm_b, w_out, ln_attn_g, ln_mem_g, w_q, w_kv, w_o, ln_ffn_g, w_gate_up, w_down, ln_final_g):
    bsz, seq, d = x.shape
    assert d == D_MODEL and seq % SEQ_TILE == 0
    assert mem.shape == (bsz, MEM_LEN, D_MODEL)
    n_tiles = seq // SEQ_TILE
    row = lambda g: g.reshape(1, -1)

    mix_rows = D_MODEL // bsz
    gu_rows = D_MODEL // (bsz * n_tiles)
    down_steps = 2
    down_rows = D_FF * down_steps // (bsz * n_tiles)
    for rows, total in ((mix_rows, D_MODEL), (gu_rows, D_MODEL), (down_rows, D_FF)):
        assert rows % BF16_SUBLANES == 0 and total % rows == 0
    assert down_rows * (bsz * n_tiles // down_steps) == D_FF

    hm = X_HEADS * MEM_LEN
    wk, vo, w_in16, w_out16 = pl.pallas_call(
        _kv_kernel,
        grid=(bsz,),
        in_specs=[
            pl.BlockSpec((1, MEM_LEN, D_MODEL), lambda b: (b, 0, 0)),
            _const_spec((1, D_MODEL)),
            _const_spec((D_MODEL, 2 * D_MODEL)),
            _const_spec((D_MODEL, D_MODEL)),
            _const_spec((D_MODEL, D_MODEL)),
            pl.BlockSpec((mix_rows, IN_TOTAL), lambda b: (b, 0)),
            pl.BlockSpec((mix_rows, D_MODEL), lambda b: (b, 0)),
        ],
        out_specs=[
            pl.BlockSpec((1, D_MODEL, hm), lambda b: (b, 0, 0)),
            pl.BlockSpec((1, hm, D_MODEL), lambda b: (b, 0, 0)),
            pl.BlockSpec((mix_rows, IN_TOTAL), lambda b: (b, 0)),
            pl.BlockSpec((mix_rows, D_MODEL), lambda b: (b, 0)),
        ],
        out_shape=[
            jax.ShapeDtypeStruct((bsz, D_MODEL, hm), BF16),
            jax.ShapeDtypeStruct((bsz, hm, D_MODEL), BF16),
            jax.ShapeDtypeStruct(w_in.shape, BF16),
            jax.ShapeDtypeStruct(w_out.shape, BF16),
        ],
        compiler_params=pltpu.CompilerParams(
            dimension_semantics=("arbitrary",), vmem_limit_bytes=VMEM_LIMIT_BYTES),
        name="kv_proj",
    )(mem, row(ln_mem_g), w_kv.astype(BF16), w_q.astype(BF16), w_o.astype(BF16), w_in, w_out)

    x_spec = pl.BlockSpec((1, SEQ_TILE, D_MODEL), lambda b, s: (b, s, 0))
    gu_spec = pl.BlockSpec((gu_rows, 2 * D_FF), lambda b, s: (b * n_tiles + s, 0))
    down_spec = pl.BlockSpec((down_rows, D_MODEL), lambda b, s: ((b * n_tiles + s) // down_steps, 0))
    x, w_gu16, w_down16 = pl.pallas_call(
        _mixer_attn_kernel,
        grid=(bsz, n_tiles),
        in_specs=[
            x_spec,
            pl.BlockSpec((1, D_MODEL, hm), lambda b, s: (b, 0, 0)),
            pl.BlockSpec((1, hm, D_MODEL), lambda b, s: (b, 0, 0)),
            _const_spec((1, D_MODEL)),
            _const_spec((D_MODEL, IN_TOTAL)),
            _const_spec((1, A_WIDTH)),
            _const_spec((1, A_WIDTH)),
            _const_spec((A_HEADS, CHUNK, CHUNK)),
            _const_spec((CHUNK, A_HEADS)),
            _const_spec((3, B_WIDTH)),
            _const_spec((1, A_WIDTH)),
            _const_spec((1, B_WIDTH)),
            _const_spec((D_MODEL, D_MODEL)),
            _const_spec((1, D_MODEL)),
            gu_spec,
            down_spec,
        ],
        out_specs=[x_spec, gu_spec, down_spec],
        out_shape=[
            jax.ShapeDtypeStruct(x.shape, F32),
            jax.ShapeDtypeStruct(w_gate_up.shape, BF16),
            jax.ShapeDtypeStruct(w_down.shape, BF16),
        ],
        scratch_shapes=[pltpu.VMEM((CARRY_ROWS, B_WIDTH), F32)],
        compiler_params=pltpu.CompilerParams(
            dimension_semantics=("arbitrary", "arbitrary"), vmem_limit_bytes=VMEM_LIMIT_BYTES),
        name="mixer_attn",
    )(x, wk, vo, row(ln_mix_g), w_in16, row(sgu_ln_g), row(sgu_ln_b), w_spatial,
      b_spatial.T, conv_w, row(grp_norm_a), row(grp_norm_b), w_out16, row(ln_attn_g), w_gate_up, w_down)

    ffn_x_spec = pl.BlockSpec((1, FFN_SEQ_TILE, D_MODEL), lambda b, s: (b, s, 0))
    return pl.pallas_call(
        _ffn_kernel,
        grid=(bsz, seq // FFN_SEQ_TILE),
        in_specs=[
            ffn_x_spec,
            _const_spec((1, D_MODEL)),
            _const_spec((D_MODEL, 2 * D_FF)),
            _const_spec((D_FF, D_MODEL)),
            _const_spec((1, D_MODEL)),
        ],
        out_specs=ffn_x_spec,
        out_shape=jax.ShapeDtypeStruct(x.shape, F32),
        compiler_params=pltpu.CompilerParams(
            dimension_semantics=("arbitrary", "arbitrary"), vmem_limit_bytes=FFN_VMEM_LIMIT_BYTES),
        name="ffn",
    )(x, row(ln_ffn_g), w_gu16, w_down16, row(ln_final_g))
```

```python
import functools
import math

import jax
import jax.numpy as jnp
from jax import lax
from jax.experimental import pallas as pl
from jax.experimental.pallas import tpu as pltpu

D_MODEL = 1024
MEM_LEN = 256
A_WIDTH = 512
B_WIDTH = 512
A_HEADS = 4
A_HEAD_DIM = A_WIDTH // A_HEADS
CHUNK = 128
IN_A = 2 * A_WIDTH
IN_TOTAL = IN_A + 3 * B_WIDTH
X_HEADS = 4
X_HEAD_DIM = D_MODEL // X_HEADS
D_FF = 2816
EPS = 1e-6

MXU_COLS = 256
SEQ_TILE = 1024
ROW_BLOCK = 512
FFN_SEQ_TILE = 1024
FFN_ROW_BLOCK = 256
FF_CHUNKS = ((0, 6 * MXU_COLS), (6 * MXU_COLS, 5 * MXU_COLS))
CARRY_ROWS = 8
BF16_SUBLANES = 16
VMEM_LIMIT_BYTES = 56 * 1024 * 1024
FFN_VMEM_LIMIT_BYTES = VMEM_LIMIT_BYTES

assert sum(w for _, w in FF_CHUNKS) == D_FF and SEQ_TILE == 2 * ROW_BLOCK and ROW_BLOCK % CHUNK == 0

COLS_U = (0, A_WIDTH)
COLS_V = (A_WIDTH, IN_A)
COLS_GATE_B = (IN_A, IN_A + B_WIDTH)
COLS_GATE_C = (IN_A + B_WIDTH, IN_A + 2 * B_WIDTH)
COLS_VAL = (IN_A + 2 * B_WIDTH, IN_TOTAL)

BF16 = jnp.bfloat16
F32 = jnp.float32


_dot = functools.partial(jnp.dot, preferred_element_type=F32)


def _gelu_tanh(x):
    c = math.sqrt(2.0 / math.pi)
    k = 0.044715
    log2e = 1.0 / math.log(2.0)
    t = x * ((-2.0 * c * log2e) + (-2.0 * c * k * log2e) * (x * x))
    return x / (1.0 + jnp.exp2(t))


def _shift_rows(z, prev, k):
    rolled = pltpu.roll(z, k, 0)
    r = lax.broadcasted_iota(jnp.int32, prev.shape, 0)
    head = jnp.where(r < k, pltpu.roll(prev, k, 0), rolled[:CARRY_ROWS, :])
    return jnp.concatenate([head, rolled[CARRY_ROWS:, :]], axis=0)


def _inv_rms(x):
    return lax.rsqrt(jnp.mean(x * x, axis=-1, keepdims=True) + EPS)


def _rms(x, g):
    return x * _inv_rms(x) * g


def _cast_chunks(src_refs, dst_refs):
    for src, dst in zip(src_refs, dst_refs):
        dst[...] = src[...].astype(BF16)


def _kv_kernel(mem_ref, g_ref, wkv_ref, wq_ref, wo_ref, w_in32, w_out32, wk_ref, vo_ref, w_in16, w_out16, w_v_t16):
    _cast_chunks((w_in32, w_out32), (w_in16, w_out16))
    w_v_t16[...] = jnp.transpose(w_in32[:, COLS_V[0]:COLS_V[1]]).astype(BF16)
    _kv_fold(mem_ref, g_ref, wkv_ref, wq_ref, wo_ref, wk_ref, vo_ref)


def _kv_fold(mem_ref, g_ref, wkv_ref, wq_ref, wo_ref, wk_ref, vo_ref):
    memn = _rms(mem_ref[0], g_ref[...]).astype(BF16)
    k = _dot(memn, wkv_ref[:, :D_MODEL])
    v = _dot(memn, wkv_ref[:, D_MODEL:]).astype(BF16)
    kt = (k * (X_HEAD_DIM ** -0.5)).T.astype(BF16)
    for hd in range(X_HEADS):
        sl = slice(hd * X_HEAD_DIM, (hd + 1) * X_HEAD_DIM)
        ml = slice(hd * MEM_LEN, (hd + 1) * MEM_LEN)
        wk_ref[0, :, ml] = _dot(wq_ref[:, sl], kt[sl, :]).astype(BF16)
        vo_ref[0, ml, :] = _dot(v[:, sl], wo_ref[sl, :]).astype(BF16)


def _interleave(stage_gens, stagger):
    gens = list(stage_gens)
    live = [True] * len(gens)
    t = 0
    while any(live):
        for i, gen in enumerate(gens):
            if live[i] and t >= stagger * i:
                live[i] = next(gen, None) is not None
        t += 1


def _emit_in_order(stage_gens, order):
    gens = list(stage_gens)
    for i in order:
        next(gens[i], None)
    for gen in gens:
        assert next(gen, None) is None


MIXER_STAGE_ORDER = (0, 0, 1, 0, 1, 0, 1, 0, 0, 0, 1, 0, 0, 1, 1, 1, 1, 1)


def _mixer_attn_stages(x_ref, o_ref, carry, r0, wk_ref, vo_ref, ln_mix_g, w_in, w_v_t, sgu_g, sgu_b, w_heads_t,
                       b_all, conv_w, gn_a, gn_b, w_out, ln_attn_g):
    rb = ROW_BLOCK
    x = x_ref[0, r0:r0 + rb, :]
    xg = (x * ln_mix_g[...]).astype(BF16)
    inv = _inv_rms(x)
    in_proj = lambda cols: _dot(xg, w_in[:, cols[0]:cols[1]]) * inv
    inv_row = jnp.transpose(jnp.broadcast_to(inv, (rb, CHUNK)))[0:1, :]
    yield True

    hv_t = lax.dot_general(w_v_t[...], xg, (((1,), (1,)), ((), ())), preferred_element_type=F32) * inv_row
    hu = in_proj(COLS_U)
    hb = in_proj(COLS_GATE_B)
    hc = in_proj(COLS_GATE_C)
    hval = in_proj(COLS_VAL)
    yield True

    v_t = _gelu_tanh(hv_t)
    mu = jnp.mean(v_t, axis=0, keepdims=True)
    vc = v_t - mu
    var = jnp.mean(vc * vc, axis=0, keepdims=True)
    v_t = (vc * lax.rsqrt(var + EPS) * sgu_g[...] + sgu_b[...]).astype(BF16)
    n_chunks = rb // CHUNK
    head_rows = []
    for hd in range(A_HEADS):
        hs = slice(hd * A_HEAD_DIM, (hd + 1) * A_HEAD_DIM)
        lhs = jnp.concatenate([v_t[hs, c * CHUNK:(c + 1) * CHUNK] for c in range(n_chunks)], axis=0)
        out = _dot(lhs, w_heads_t[hd]) + b_all[hd:hd + 1, :]
        head_rows.append(jnp.concatenate(
            [out[c * A_HEAD_DIM:(c + 1) * A_HEAD_DIM, :] for c in range(n_chunks)], axis=1))
    mixed = jnp.transpose(jnp.concatenate(head_rows, axis=0))
    yield True

    y_a = _rms(_gelu_tanh(hu) * mixed, gn_a[...]).astype(BF16)
    z = hc * hval
    prev = carry['z']
    carry['z'] = z[rb - CARRY_ROWS:, :]
    z1 = _shift_rows(z, prev, 1)
    z2 = _shift_rows(z, prev, 2)
    cw = conv_w[...]
    conv = cw[0:1, :] * z2 + cw[1:2, :] * z1 + cw[2:3, :] * z
    y_b = _rms(hb * conv, gn_b[...]).astype(BF16)
    y = jnp.concatenate([y_a, y_b], axis=-1)
    yield True

    x = x + _dot(y, w_out[...])
    yield True

    xg = (x * ln_attn_g[...]).astype(BF16)
    inv = _inv_rms(x)
    yield True

    s_all = _dot(xg, wk_ref[0]) * inv
    yield True

    probs = []
    for hd in range(X_HEADS):
        s = s_all[:, hd * MEM_LEN:(hd + 1) * MEM_LEN]
        p = jnp.exp(s - jnp.max(s, axis=-1, keepdims=True))
        probs.append(p / jnp.sum(p, axis=-1, keepdims=True))
    p_all = jnp.concatenate(probs, axis=-1).astype(BF16)
    yield True

    o_ref[0, r0:r0 + rb, :] = x + _dot(p_all, vo_ref[0])


def _mixer_attn_kernel(x_ref, wk_ref, vo_ref, ln_mix_g, w_in, w_v_t, sgu_g, sgu_b, w_sp, b_sp, conv_w,
                       gn_a, gn_b, w_out, ln_attn_g, w_gu32, w_down32, o_ref, w_gu16, w_down16, zc_ref):
    _cast_chunks((w_gu32, w_down32), (w_gu16, w_down16))

    @pl.when(pl.program_id(1) == 0)
    def _():
        zc_ref[...] = jnp.zeros_like(zc_ref)

    row = lax.broadcasted_iota(jnp.int32, (CHUNK, CHUNK), 0)
    col = lax.broadcasted_iota(jnp.int32, (CHUNK, CHUNK), 1)
    w_heads_t = [jnp.transpose(jnp.where(row >= col, w_sp[hd], 0.0)).astype(BF16) for hd in range(A_HEADS)]
    b_all = b_sp[...]

    carry = {'z': zc_ref[...]}
    _emit_in_order(
        (_mixer_attn_stages(x_ref, o_ref, carry, r0, wk_ref, vo_ref, ln_mix_g, w_in, w_v_t, sgu_g, sgu_b, w_heads_t,
                            b_all, conv_w, gn_a, gn_b, w_out, ln_attn_g)
         for r0 in range(0, x_ref.shape[1], ROW_BLOCK)),
        MIXER_STAGE_ORDER)
    zc_ref[...] = carry['z']


def _ffn_stages(x_ref, o_ref, r0, ln_ffn_g, w_gu, w_down, ln_final_g):
    x = x_ref[0, r0:r0 + FFN_ROW_BLOCK, :]
    xg = (x * ln_ffn_g[...]).astype(BF16)
    inv = _inv_rms(x)
    yield True
    acc = x
    for lo, width in FF_CHUNKS:
        g = _dot(xg, w_gu[:, lo:lo + width]) * inv
        u = _dot(xg, w_gu[:, D_FF + lo:D_FF + lo + width]) * inv
        yield True
        act = (jax.nn.silu(g) * u).astype(BF16)
        yield True
        acc = acc + _dot(act, w_down[lo:lo + width, :])
    yield True
    o_ref[0, r0:r0 + FFN_ROW_BLOCK, :] = _rms(acc, ln_final_g[...])


def _ffn_kernel(x_ref, ln_ffn_g, w_gu, w_down, ln_final_g, o_ref):
    _interleave(
        (_ffn_stages(x_ref, o_ref, r0, ln_ffn_g, w_gu, w_down, ln_final_g)
         for r0 in range(0, x_ref.shape[1], FFN_ROW_BLOCK)),
        stagger=1)


def _const_spec(shape):
    zeros = (0,) * len(shape)
    return pl.BlockSpec(shape, lambda *_: zeros, pipeline_mode=pl.Buffered(1))


def kernel(x, mem, ln_mix_g, w_in, sgu_ln_g, sgu_ln_b, w_spatial, b_spatial, conv_w, grp_norm_a, grp_norm_b, w_out, ln_attn_g, ln_mem_g, w_q, w_kv, w_o, ln_ffn_g, w_gate_up, w_down, ln_final_g):
    bsz, seq, d = x.shape
    assert d == D_MODEL and seq % SEQ_TILE == 0
    assert mem.shape == (bsz, MEM_LEN, D_MODEL)
    n_tiles = seq // SEQ_TILE
    row = lambda g: g.reshape(1, -1)

    mix_rows = D_MODEL // bsz
    gu_rows = D_MODEL // (bsz * n_tiles)
    down_steps = 2
    down_rows = D_FF * down_steps // (bsz * n_tiles)
    for rows, total in ((mix_rows, D_MODEL), (gu_rows, D_MODEL), (down_rows, D_FF)):
        assert rows % BF16_SUBLANES == 0 and total % rows == 0
    assert down_rows * (bsz * n_tiles // down_steps) == D_FF

    hm = X_HEADS * MEM_LEN
    wk, vo, w_in16, w_out16, w_v_t16 = pl.pallas_call(
        _kv_kernel,
        grid=(bsz,),
        in_specs=[
            pl.BlockSpec((1, MEM_LEN, D_MODEL), lambda b: (b, 0, 0)),
            _const_spec((1, D_MODEL)),
            _const_spec((D_MODEL, 2 * D_MODEL)),
            _const_spec((D_MODEL, D_MODEL)),
            _const_spec((D_MODEL, D_MODEL)),
            pl.BlockSpec((mix_rows, IN_TOTAL), lambda b: (b, 0)),
            pl.BlockSpec((mix_rows, D_MODEL), lambda b: (b, 0)),
        ],
        out_specs=[
            pl.BlockSpec((1, D_MODEL, hm), lambda b: (b, 0, 0)),
            pl.BlockSpec((1, hm, D_MODEL), lambda b: (b, 0, 0)),
            pl.BlockSpec((mix_rows, IN_TOTAL), lambda b: (b, 0)),
            pl.BlockSpec((mix_rows, D_MODEL), lambda b: (b, 0)),
            pl.BlockSpec((A_WIDTH, mix_rows), lambda b: (0, b)),
        ],
        out_shape=[
            jax.ShapeDtypeStruct((bsz, D_MODEL, hm), BF16),
            jax.ShapeDtypeStruct((bsz, hm, D_MODEL), BF16),
            jax.ShapeDtypeStruct(w_in.shape, BF16),
            jax.ShapeDtypeStruct(w_out.shape, BF16),
            jax.ShapeDtypeStruct((A_WIDTH, D_MODEL), BF16),
        ],
        compiler_params=pltpu.CompilerParams(
            dimension_semantics=("arbitrary",), vmem_limit_bytes=VMEM_LIMIT_BYTES),
        name="kv_proj",
    )(mem, row(ln_mem_g), w_kv.astype(BF16), w_q.astype(BF16), w_o.astype(BF16), w_in, w_out)

    x_spec = pl.BlockSpec((1, SEQ_TILE, D_MODEL), lambda b, s: (b, s, 0))
    gu_spec = pl.BlockSpec((gu_rows, 2 * D_FF), lambda b, s: (b * n_tiles + s, 0))
    down_spec = pl.BlockSpec((down_rows, D_MODEL), lambda b, s: ((b * n_tiles + s) // down_steps, 0))
    x, w_gu16, w_down16 = pl.pallas_call(
        _mixer_attn_kernel,
        grid=(bsz, n_tiles),
        in_specs=[
            x_spec,
            pl.BlockSpec((1, D_MODEL, hm), lambda b, s: (b, 0, 0)),
            pl.BlockSpec((1, hm, D_MODEL), lambda b, s: (b, 0, 0)),
            _const_spec((1, D_MODEL)),
            _const_spec((D_MODEL, IN_TOTAL)),
            _const_spec((A_WIDTH, D_MODEL)),
            _const_spec((A_WIDTH, 1)),
            _const_spec((A_WIDTH, 1)),
            _const_spec((A_HEADS, CHUNK, CHUNK)),
            _const_spec((A_HEADS, CHUNK)),
            _const_spec((3, B_WIDTH)),
            _const_spec((1, A_WIDTH)),
            _const_spec((1, B_WIDTH)),
            _const_spec((D_MODEL, D_MODEL)),
            _const_spec((1, D_MODEL)),
            gu_spec,
            down_spec,
        ],
        out_specs=[x_spec, gu_spec, down_spec],
        out_shape=[
            jax.ShapeDtypeStruct(x.shape, F32),
            jax.ShapeDtypeStruct(w_gate_up.shape, BF16),
            jax.ShapeDtypeStruct(w_down.shape, BF16),
        ],
        scratch_shapes=[pltpu.VMEM((CARRY_ROWS, B_WIDTH), F32)],
        compiler_params=pltpu.CompilerParams(
            dimension_semantics=("arbitrary", "arbitrary"), vmem_limit_bytes=VMEM_LIMIT_BYTES),
        name="mixer_attn",
    )(x, wk, vo, row(ln_mix_g), w_in16, w_v_t16, sgu_ln_g.reshape(-1, 1), sgu_ln_b.reshape(-1, 1), w_spatial,
      b_spatial, conv_w, row(grp_norm_a), row(grp_norm_b), w_out16, row(ln_attn_g), w_gate_up, w_down)

    ffn_x_spec = pl.BlockSpec((1, FFN_SEQ_TILE, D_MODEL), lambda b, s: (b, s, 0))
    return pl.pallas_call(
        _ffn_kernel,
        grid=(bsz, seq // FFN_SEQ_TILE),
        in_specs=[
            ffn_x_spec,
            _const_spec((1, D_MODEL)),
            _const_spec((D_MODEL, 2 * D_FF)),
            _const_spec((D_FF, D_MODEL)),
            _const_spec((1, D_MODEL)),
        ],
        out_specs=ffn_x_spec,
        out_shape=jax.ShapeDtypeStruct(x.shape, F32),
        compiler_params=pltpu.CompilerParams(
            dimension_semantics=("arbitrary", "arbitrary"), vmem_limit_bytes=FFN_VMEM_LIMIT_BYTES),
        name="ffn",
    )(x, row(ln_ffn_g), w_gu16, w_down16, row(ln_final_g))
```
